```python
import jax, jax.numpy as jnp
from jax import lax
import numpy as np

D_MODEL = 2048
BATCH = 1
SEQ = 8192
DEPTH = 1

CONV_WIDTH = 1024
CONV_KERNEL = 3
ATTN_HEADS = 16
ATTN_HEAD_DIM = 64
ATTN_WIDTH = ATTN_HEADS * ATTN_HEAD_DIM
Q_BLOCK = 128
IN_WIDTH = 3 * CONV_WIDTH + 3 * ATTN_WIDTH + 2 * D_MODEL
PEER_HEADS = 8
PEER_KEYS = 128
PEER_EXPERTS = PEER_KEYS * PEER_KEYS
PEER_QUERY_DIM = 256
PEER_HALF = PEER_QUERY_DIM // 2
PEER_TOPK = 16
PEER_TOKEN_BLOCK = 128
RMS_EPS = 1e-6

kernel_name = "hybrid_conv_stickbreak_peer"


def rmsnorm(x, gain):
    xf = x.astype(jnp.float32)
    inv = lax.rsqrt(jnp.mean(xf * xf, axis=-1, keepdims=True) + RMS_EPS)
    return (xf * inv * gain.astype(jnp.float32)).astype(x.dtype)


def short_conv_mixer(b_gate, c_gate, h, conv_w):
    z = c_gate * h
    y = lax.conv_general_dilated(
        z, conv_w[:, None, :], window_strides=(1,),
        padding=[(CONV_KERNEL - 1, 0)],
        dimension_numbers=("NWC", "WIO", "NWC"),
        feature_group_count=z.shape[-1])
    return b_gate * y


def stick_breaking_attention(q, k, v):
    b, t, h, dh = q.shape
    nb = t // Q_BLOCK
    scale = dh ** -0.5
    kf = k.astype(jnp.float32)
    vf = v.astype(jnp.float32)
    q_blocks = jnp.moveaxis(q.reshape(b, nb, Q_BLOCK, h, dh), 1, 0)
    key_pos = jnp.arange(t)

    def one_block(args):
        qb, blk = args
        q_pos = blk * Q_BLOCK + jnp.arange(Q_BLOCK)
        mask = key_pos[None, :] < q_pos[:, None]
        z = jnp.einsum("bqhd,bkhd->bhqk", qb.astype(jnp.float32), kf) * scale
        log_keep = jnp.where(mask, jax.nn.log_sigmoid(-z), 0.0)
        suffix = lax.cumsum(log_keep, axis=3, reverse=True)
        suffix_excl = jnp.concatenate(
            [suffix[..., 1:], jnp.zeros_like(suffix[..., :1])], axis=-1)
        weights = jnp.where(mask, jnp.exp(jax.nn.log_sigmoid(z) + suffix_excl), 0.0)
        return jnp.einsum("bhqk,bkhd->bqhd", weights, vf)

    out = lax.map(one_block, (q_blocks, jnp.arange(nb)))
    return jnp.moveaxis(out, 0, 1).reshape(b, t, h, dh).astype(q.dtype)


def peer_ffn(x, wq, subkeys, u_table, v_table):
    b, t, d = x.shape
    q = jnp.einsum("btd,dq->btq", x, wq).reshape(b, t, PEER_HEADS, 2, PEER_HALF)
    scores = jnp.einsum("bthsc,snc->bthsn", q, subkeys)
    top_s, top_i = lax.top_k(scores, PEER_TOPK)
    cand_s = top_s[..., 0, :, None] + top_s[..., 1, None, :]
    cand_i = top_i[..., 0, :, None] * PEER_KEYS + top_i[..., 1, None, :]
    cand_s = cand_s.reshape(b, t, PEER_HEADS, PEER_TOPK * PEER_TOPK)
    cand_i = cand_i.reshape(b, t, PEER_HEADS, PEER_TOPK * PEER_TOPK)
    sel_s, sel_pos = lax.top_k(cand_s, PEER_TOPK)
    sel_i = jnp.take_along_axis(cand_i, sel_pos, axis=-1)
    gate = jax.nn.softmax(sel_s.astype(jnp.float32), axis=-1)

    n = b * t
    nb = n // PEER_TOKEN_BLOCK
    xs = x.reshape(nb, PEER_TOKEN_BLOCK, d)
    idx = sel_i.reshape(nb, PEER_TOKEN_BLOCK, PEER_HEADS, PEER_TOPK)
    gs = gate.reshape(nb, PEER_TOKEN_BLOCK, PEER_HEADS, PEER_TOPK)

    def one_block(args):
        xb, ib, gb = args
        u = u_table[ib]
        hidden = jnp.einsum("td,thkd->thk", xb, u)
        act = jax.nn.gelu(hidden.astype(jnp.float32), approximate=False) * gb
        v = v_table[ib]
        return jnp.einsum("thk,thkd->td", act.astype(xb.dtype), v)

    out = lax.map(one_block, (xs, idx, gs))
    return out.reshape(b, t, d)


def setup_inputs(seed: int = 0) -> dict:
    key = jax.random.key(seed)
    ks = jax.random.split(key, 14)
    f32 = jnp.float32
    x = jax.random.normal(ks[0], (BATCH, SEQ, D_MODEL), f32)
    norm_mix = 1.0 + 0.02 * jax.random.normal(ks[1], (DEPTH, D_MODEL), f32)
    w_in = jax.random.normal(ks[2], (DEPTH, D_MODEL, IN_WIDTH), f32) * D_MODEL ** -0.5
    conv_w = jax.random.normal(ks[3], (DEPTH, CONV_KERNEL, CONV_WIDTH), f32) * CONV_KERNEL ** -0.5
    w_conv_out = jax.random.normal(ks[4], (DEPTH, CONV_WIDTH, D_MODEL), f32) * CONV_WIDTH ** -0.5
    w_attn_out = jax.random.normal(ks[5], (DEPTH, ATTN_WIDTH, D_MODEL), f32) * ATTN_WIDTH ** -0.5
    w_out = jax.random.normal(ks[6], (DEPTH, D_MODEL, D_MODEL), f32) * D_MODEL ** -0.5
    norm_ffn = 1.0 + 0.02 * jax.random.normal(ks[7], (DEPTH, D_MODEL), f32)
    peer_wq = jax.random.normal(ks[8], (DEPTH, D_MODEL, PEER_HEADS * PEER_QUERY_DIM), f32) * D_MODEL ** -0.5
    peer_subkeys = jax.random.normal(ks[9], (DEPTH, 2, PEER_KEYS, PEER_HALF), f32) * PEER_HALF ** -0.5
    peer_u = jax.random.normal(ks[10], (DEPTH, PEER_EXPERTS, D_MODEL), f32) * D_MODEL ** -0.5
    peer_v = jax.random.normal(ks[11], (DEPTH, PEER_EXPERTS, D_MODEL), f32) * PEER_HEADS ** -0.5
    norm_final = 1.0 + 0.02 * jax.random.normal(ks[12], (D_MODEL,), f32)
    return {"x": x, "norm_mix": norm_mix, "w_in": w_in, "conv_w": conv_w,
            "w_conv_out": w_conv_out, "w_attn_out": w_attn_out, "w_out": w_out,
            "norm_ffn": norm_ffn, "peer_wq": peer_wq, "peer_subkeys": peer_subkeys,
            "peer_u": peer_u, "peer_v": peer_v, "norm_final": norm_final}


def reference(x, norm_mix, w_in, conv_w, w_conv_out, w_attn_out, w_out,
              norm_ffn, peer_wq, peer_subkeys, peer_u, peer_v, norm_final):
    b, t, d = x.shape
    splits = [CONV_WIDTH, 2 * CONV_WIDTH, 3 * CONV_WIDTH,
              3 * CONV_WIDTH + ATTN_WIDTH, 3 * CONV_WIDTH + 2 * ATTN_WIDTH,
              3 * CONV_WIDTH + 3 * ATTN_WIDTH, 3 * CONV_WIDTH + 3 * ATTN_WIDTH + D_MODEL]
    for layer in range(DEPTH):
        xn = rmsnorm(x, norm_mix[layer])
        proj = jnp.einsum("btd,de->bte", xn, w_in[layer])
        cb, cc, ch, q, k, v, g_conv, g_attn = jnp.split(proj, splits, axis=-1)
        y_conv = short_conv_mixer(cb, cc, ch, conv_w[layer])
        y_conv = jnp.einsum("btc,cd->btd", y_conv, w_conv_out[layer])
        y_attn = stick_breaking_attention(
            q.reshape(b, t, ATTN_HEADS, ATTN_HEAD_DIM),
            k.reshape(b, t, ATTN_HEADS, ATTN_HEAD_DIM),
            v.reshape(b, t, ATTN_HEADS, ATTN_HEAD_DIM)).reshape(b, t, ATTN_WIDTH)
        y_attn = jnp.einsum("bta,ad->btd", y_attn, w_attn_out[layer])
        merged = jax.nn.sigmoid(g_conv) * y_conv + jax.nn.sigmoid(g_attn) * y_attn
        x = x + jnp.einsum("btd,de->bte", merged, w_out[layer])
        xn = rmsnorm(x, norm_ffn[layer])
        x = x + peer_ffn(xn, peer_wq[layer], peer_subkeys[layer], peer_u[layer], peer_v[layer])
    return rmsnorm(x, norm_final)
```

```python
import functools
import math

import jax
import jax.numpy as jnp
from jax import lax
from jax.experimental import pallas as pl
from jax.experimental.pallas import tpu as pltpu

RMS_EPS = 1e-6
LANES = 128
SUBLANES = 8
VMEM_LIMIT = 56 * 1024 * 1024
PEER_TOPK = 16
NEG_INF = float("-inf")
INV_SQRT2 = 0.7071067811865476


def _cparams(*sem):
    return pltpu.CompilerParams(dimension_semantics=sem, vmem_limit_bytes=VMEM_LIMIT)


def _rms_matmul_kernel(x_ref, g_ref, w_ref, o_ref, *rest, scale, emit_xn):
    if emit_xn:
        xn_out_ref, xn_ref = rest
    else:
        (xn_ref,) = rest

    @pl.when(pl.program_id(1) == 0)
    def _():
        x = x_ref[...]
        inv = lax.rsqrt(jnp.mean(x * x, axis=-1, keepdims=True) + RMS_EPS)
        xn = (x * inv * g_ref[...]).astype(jnp.bfloat16)
        xn_ref[...] = xn
        if emit_xn:
            xn_out_ref[...] = xn

    acc = jnp.dot(xn_ref[...], w_ref[...], preferred_element_type=jnp.float32)
    if scale != 1.0:
        acc = acc * scale
    o_ref[...] = acc.astype(o_ref.dtype)


def rms_matmul(x, gain, w, col0, ncols, out_dtype, *, scale=1.0, emit_xn=False, tm=512, tn=512):
    t, d = x.shape
    tm = min(tm, t)
    tn = min(tn, ncols)
    assert t % tm == 0 and ncols % tn == 0 and col0 % tn == 0
    jb0 = col0 // tn
    out_shape = [jax.ShapeDtypeStruct((t, ncols), out_dtype)]
    out_specs = [pl.BlockSpec((tm, tn), lambda i, j: (i, j))]
    if emit_xn:
        out_shape.append(jax.ShapeDtypeStruct((t, d), jnp.bfloat16))
        out_specs.append(pl.BlockSpec((tm, d), lambda i, j: (i, 0)))
    res = pl.pallas_call(
        functools.partial(_rms_matmul_kernel, scale=scale, emit_xn=emit_xn),
        grid=(t // tm, ncols // tn),
        in_specs=[
            pl.BlockSpec((tm, d), lambda i, j: (i, 0)),
            pl.BlockSpec((1, d), lambda i, j: (0, 0)),
            pl.BlockSpec((d, tn), lambda i, j: (0, jb0 + j)),
        ],
        out_specs=out_specs,
        out_shape=out_shape,
        scratch_shapes=[pltpu.VMEM((tm, d), jnp.bfloat16)],
        compiler_params=_cparams("parallel", "arbitrary"),
        name="rms_matmul",
    )(x, gain.reshape(1, d), w)
    return res if emit_xn else res[0]


def _conv_kernel(cb_ref, cc_ref, ch_ref, cch_ref, chh_ref, w_ref, o_ref, z_ref):
    tm = cb_ref.shape[0]
    z_halo = cch_ref[...] * chh_ref[...]
    z_ref[0:SUBLANES, :] = jnp.where(pl.program_id(0) == 0, 0.0, z_halo)
    z = cc_ref[...] * ch_ref[...]
    z_ref[SUBLANES:, :] = z
    w = w_ref[...]
    k = w.shape[0]
    y = z * w[k - 1:k, :]
    for j in range(1, k):
        y = y + z_ref[pl.ds(SUBLANES - j, tm), :] * w[k - 1 - j:k - j, :]
    o_ref[...] = (cb_ref[...] * y).astype(o_ref.dtype)


def conv_mixer(pc, conv_w, *, tm=512):
    t, c3 = pc.shape
    c = c3 // 3
    k = conv_w.shape[0]
    assert k - 1 <= SUBLANES
    tm = min(tm, t)
    assert t % tm == 0 and tm % SUBLANES == 0
    hb = tm // SUBLANES

    def halo_map(col):
        return lambda i: (jnp.maximum(i * hb - 1, 0), col)

    return pl.pallas_call(
        _conv_kernel,
        grid=(t // tm,),
        in_specs=[
            pl.BlockSpec((tm, c), lambda i: (i, 0)),
            pl.BlockSpec((tm, c), lambda i: (i, 1)),
            pl.BlockSpec((tm, c), lambda i: (i, 2)),
            pl.BlockSpec((SUBLANES, c), halo_map(1)),
            pl.BlockSpec((SUBLANES, c), halo_map(2)),
            pl.BlockSpec((k, c), lambda i: (0, 0)),
        ],
        out_specs=pl.BlockSpec((tm, c), lambda i: (i, 0)),
        out_shape=jax.ShapeDtypeStruct((t, c), jnp.bfloat16),
        scratch_shapes=[pltpu.VMEM((tm + SUBLANES, c), jnp.float32)],
        compiler_params=_cparams("parallel"),
        name="conv_mixer",
    )(pc, pc, pc, pc, pc, conv_w)


ATT_BK = 128


def _attn_kernel(q_ref, k_ref, v_ref, tri_ref, o_ref, acc_ref, car_ref, *, dh):
    bq = q_ref.shape[0]
    nsub = bq // ATT_BK
    i = pl.program_id(1)
    lane = lax.broadcasted_iota(jnp.int32, (1, LANES), 1)
    head0 = lane < dh
    q2 = q_ref[...]
    zero = jnp.zeros_like(q2)
    qh = (jnp.where(head0, q2, zero), jnp.where(head0, zero, q2))
    tri = tri_ref[...]

    acc_ref[...] = jnp.zeros_like(acc_ref)
    car_ref[...] = jnp.zeros_like(car_ref)

    def block(kb, vb, mask):
        for h in range(2):
            z = lax.dot_general(qh[h], kb, (((1,), (1,)), ((), ())),
                                preferred_element_type=jnp.float32)
            nz = -z
            lk = jnp.minimum(nz, 0.0) - jnp.log(1.0 + jnp.exp(jnp.minimum(z, nz)))
            if mask is not None:
                lk = jnp.where(mask, lk, 0.0)
            hi = lk.astype(jnp.bfloat16)
            lo = (lk - hi.astype(jnp.float32)).astype(jnp.bfloat16)
            r = jnp.dot(jnp.concatenate([hi, lo], axis=1), tri,
                        preferred_element_type=jnp.float32)
            sfx = r[:, :ATT_BK]
            tot = r[:, ATT_BK:]
            car = car_ref[h]
            w = jnp.exp(z + lk + sfx + car)
            if mask is not None:
                w = jnp.where(mask, w, 0.0)
            acc_ref[h] += jnp.dot(w.astype(jnp.bfloat16), vb, preferred_element_type=jnp.float32)
            car_ref[h] = car + tot

    row = lax.broadcasted_iota(jnp.int32, (bq, ATT_BK), 0)
    col = lax.broadcasted_iota(jnp.int32, (bq, ATT_BK), 1)
    for d in range(nsub - 1, -1, -1):
        start = pl.multiple_of(i * bq + d * ATT_BK, ATT_BK)
        block(k_ref[pl.ds(start, ATT_BK), :], v_ref[pl.ds(start, ATT_BK), :],
              (col + d * ATT_BK) < row)

    def body(n, carry):
        jb = i * nsub - 1 - n
        start = pl.multiple_of(jb * ATT_BK, ATT_BK)
        block(k_ref[pl.ds(start, ATT_BK), :], v_ref[pl.ds(start, ATT_BK), :], None)
        return carry

    lax.fori_loop(0, i * nsub, body, 0)
    o_ref[...] = jnp.where(head0, acc_ref[0], acc_ref[1]).astype(o_ref.dtype)


def _tri_matrix():
    j = jnp.arange(ATT_BK)[:, None]
    s = jnp.arange(ATT_BK)[None, :]
    upper = (j > s).astype(jnp.bfloat16)
    half = jnp.concatenate([upper, jnp.ones((ATT_BK, ATT_BK), jnp.bfloat16)], axis=1)
    return jnp.concatenate([half, half], axis=0)


def attention(q, kv, n_heads, dh, *, bq=256):
    t = q.shape[0]
    assert 2 * dh == LANES and n_heads % 2 == 0
    npair = n_heads // 2
    bq = min(bq, t)
    assert t % bq == 0 and bq % ATT_BK == 0
    return pl.pallas_call(
        functools.partial(_attn_kernel, dh=dh),
        grid=(npair, t // bq),
        in_specs=[
            pl.BlockSpec((bq, LANES), lambda p, i: (i, p)),
            pl.BlockSpec((t, LANES), lambda p, i: (0, p)),
            pl.BlockSpec((t, LANES), lambda p, i: (0, npair + p)),
            pl.BlockSpec((2 * ATT_BK, 2 * ATT_BK), lambda p, i: (0, 0)),
        ],
        out_specs=pl.BlockSpec((bq, LANES), lambda p, i: (i, p)),
        out_shape=jax.ShapeDtypeStruct((t, n_heads * dh), jnp.bfloat16),
        scratch_shapes=[pltpu.VMEM((2, bq, LANES), jnp.float32),
                        pltpu.VMEM((2, bq, ATT_BK), jnp.float32)],
        compiler_params=_cparams("parallel", "parallel"),
        name="stickbreak_attention",
    )(q, kv, kv, _tri_matrix())


def _merge_kernel(yc_ref, ya_ref, wc_ref, wa_ref, gc_ref, ga_ref, o_ref):
    yc = jnp.dot(yc_ref[...], wc_ref[...], preferred_element_type=jnp.float32)
    ya = jnp.dot(ya_ref[...], wa_ref[...], preferred_element_type=jnp.float32)
    o_ref[...] = (jax.nn.sigmoid(gc_ref[...]) * yc + jax.nn.sigmoid(ga_ref[...]) * ya).astype(o_ref.dtype)


def merge(yc, ya, wc, wa, gates, *, tm=512, tn=512):
    t, c = yc.shape
    a = ya.shape[1]
    d = wc.shape[1]
    tm = min(tm, t)
    tn = min(tn, d)
    assert t % tm == 0 and d % tn == 0
    nj = d // tn
    return pl.pallas_call(
        _merge_kernel,
        grid=(t // tm, nj),
        in_specs=[
            pl.BlockSpec((tm, c), lambda i, j: (i, 0)),
            pl.BlockSpec((tm, a), lambda i, j: (i, 0)),
            pl.BlockSpec((c, tn), lambda i, j: (0, j)),
            pl.BlockSpec((a, tn), lambda i, j: (0, j)),
            pl.BlockSpec((tm, tn), lambda i, j: (i, j)),
            pl.BlockSpec((tm, tn), lambda i, j: (i, nj + j)),
        ],
        out_specs=pl.BlockSpec((tm, tn), lambda i, j: (i, j)),
        out_shape=jax.ShapeDtypeStruct((t, d), jnp.bfloat16),
        compiler_params=_cparams("parallel", "parallel"),
        name="merge_mixers",
    )(yc, ya, wc, wa, gates, gates)


def _out_proj_kernel(m_ref, w_ref, x_ref, o_ref):
    o_ref[...] = x_ref[...] + jnp.dot(m_ref[...], w_ref[...], preferred_element_type=jnp.float32)


def out_proj(merged, w, x, *, tm=512, tn=512):
    t, d = merged.shape
    n = w.shape[1]
    tm = min(tm, t)
    tn = min(tn, n)
    assert t % tm == 0 and n % tn == 0
    return pl.pallas_call(
        _out_proj_kernel,
        grid=(t // tm, n // tn),
        in_specs=[
            pl.BlockSpec((tm, d), lambda i, j: (i, 0)),
            pl.BlockSpec((d, tn), lambda i, j: (0, j)),
            pl.BlockSpec((tm, tn), lambda i, j: (i, j)),
        ],
        out_specs=pl.BlockSpec((tm, tn), lambda i, j: (i, j)),
        out_shape=jax.ShapeDtypeStruct((t, n), jnp.float32),
        compiler_params=_cparams("parallel", "parallel"),
        name="out_proj",
    )(merged, w, x)


def _scores_kernel(q_ref, k_ref, o_ref):
    nk = k_ref.shape[1]
    half = k_ref.shape[2]
    for hs in range(o_ref.shape[0]):
        q = q_ref[:, hs * half:(hs + 1) * half].astype(jnp.bfloat16)
        keys = k_ref[hs % 2]
        o_ref[hs] = lax.dot_general(keys, q, (((1,), (1,)), ((), ())),
                                    preferred_element_type=jnp.float32)


def peer_scores(qp, subkeys, n_heads, *, tm=512):
    t, qd = qp.shape
    _, nk, half = subkeys.shape
    assert qd == n_heads * 2 * half
    tm = min(tm, t)
    assert t % tm == 0
    return pl.pallas_call(
        _scores_kernel,
        grid=(t // tm,),
        in_specs=[
            pl.BlockSpec((tm, qd), lambda i: (i, 0)),
            pl.BlockSpec((2, nk, half), lambda i: (0, 0, 0)),
        ],
        out_specs=pl.BlockSpec((2 * n_heads, nk, tm), lambda i: (0, 0, i)),
        out_shape=jax.ShapeDtypeStruct((2 * n_heads, nk, t), jnp.float32),
        compiler_params=_cparams("parallel"),
        name="peer_scores",
    )(qp, subkeys)


def _extract_top(s, count):
    n = s.shape[0]
    row = lax.broadcasted_iota(jnp.int32, s.shape, 0).astype(jnp.float32)
    slot = lax.broadcasted_iota(jnp.int32, (count, s.shape[1]), 0).astype(jnp.float32)
    not_taken = float(count)

    def body(r, carry):
        rank, vals = carry
        rf = r.astype(jnp.float32)
        cur = jnp.where(rank == not_taken, s, NEG_INF)
        m = jnp.max(cur, axis=0, keepdims=True)
        first = jnp.min(jnp.where(cur == m, row, float(n)), axis=0, keepdims=True)
        rank = jnp.where(row == first, rf, rank)
        vals = jnp.where(slot == rf, m, vals)
        return rank, vals

    init = (jnp.full(s.shape, not_taken, jnp.float32), jnp.zeros((count, s.shape[1]), jnp.float32))
    return lax.fori_loop(0, count, body, init)


def _topk_kernel(s_ref, r2_ref, e2_ref, ell_ref, w_ref):
    k = PEER_TOPK
    nchunk = r2_ref.shape[1]
    for c in range(nchunk):
        s1 = s_ref[0, :, c * LANES:(c + 1) * LANES]
        s2 = s_ref[1, :, c * LANES:(c + 1) * LANES]
        rank1, a = _extract_top(s1, k)
        rank2, b = _extract_top(s2, k)
        sub = lax.broadcasted_iota(jnp.int32, (SUBLANES, LANES), 0)
        groups = [a[0:1] + b[0:SUBLANES], a[0:1] + b[SUBLANES:2 * SUBLANES]]
        for i in range(1, SUBLANES):
            groups.append(jnp.where(sub < k // (i + 1), a[i:i + 1] + b[0:SUBLANES], NEG_INF))
        groups.append(a[SUBLANES:] + b[0:1])
        cand = jnp.concatenate(groups, axis=0)
        crank, cvals = _extract_top(cand, k)
        taken = jnp.where(crank < float(k), 1.0, 0.0)
        z = jnp.sum(jnp.exp(cvals - cvals[0:1]), axis=0, keepdims=True)
        ell = jnp.zeros_like(s1)
        lim0 = jnp.sum(taken[0:2 * SUBLANES], axis=0, keepdims=True)
        ell = jnp.where(rank1 == 0.0, lim0, ell)
        for i in range(1, SUBLANES):
            lim = jnp.sum(taken[(i + 1) * SUBLANES:(i + 2) * SUBLANES], axis=0, keepdims=True)
            ell = jnp.where(rank1 == float(i), lim, ell)
        base = (SUBLANES + 1) * SUBLANES
        for j in range(SUBLANES):
            ell = jnp.where(rank1 == float(SUBLANES + j), taken[base + j:base + j + 1], ell)
        r2_ref[0, c] = rank2
        e2_ref[0, c] = jnp.exp(s2 - b[0:1])
        ell_ref[0, c] = ell
        w_ref[0, c] = jnp.exp(s1 - a[0:1]) / z


def peer_topk(scores, n_heads, *, tl=512):
    hs, nk, t = scores.shape
    assert hs == 2 * n_heads and PEER_TOPK == 2 * SUBLANES
    tl = min(tl, t)
    assert t % tl == 0 and tl % LANES == 0
    nchunk = tl // LANES
    out = jax.ShapeDtypeStruct((n_heads, t // LANES, nk, LANES), jnp.float32)
    ospec = pl.BlockSpec((1, nchunk, nk, LANES), lambda h, i: (h, i, 0, 0))
    return pl.pallas_call(
        _topk_kernel,
        grid=(n_heads, t // tl),
        in_specs=[pl.BlockSpec((2, nk, tl), lambda h, i: (h, 0, i))],
        out_specs=[ospec] * 4,
        out_shape=[out] * 4,
        compiler_params=_cparams("parallel", "parallel"),
        name="peer_topk",
    )(scores)


def _peer_dense_kernel(xn_ref, u_ref, vt_ref, r2_ref, e2_ref, ell_ref, w_ref, o_ref, h_ref, a_ref):
    n_heads, nchunk, nk, _ = r2_ref.shape
    na = ell_ref.shape[2]

    @pl.when(pl.program_id(1) == 0)
    def _():
        o_ref[...] = jnp.zeros_like(o_ref)

    h_ref[...] = lax.dot_general(u_ref[...], xn_ref[...], (((1,), (1,)), ((), ())),
                                 preferred_element_type=jnp.float32)
    for ai in range(na):
        for c in range(nchunk):
            g = jnp.zeros((nk, LANES), jnp.float32)
            for h in range(n_heads):
                lim = ell_ref[h, c, ai:ai + 1, :]
                wa = w_ref[h, c, ai:ai + 1, :]
                g = g + jnp.where(r2_ref[h, c] < lim, e2_ref[h, c], 0.0) * wa
            hid = h_ref[ai * nk:(ai + 1) * nk, c * LANES:(c + 1) * LANES]
            act = 0.5 * hid * (1.0 + lax.erf(hid * INV_SQRT2))
            a_ref[ai * nk:(ai + 1) * nk, c * LANES:(c + 1) * LANES] = (act * g).astype(jnp.bfloat16)
    o_ref[...] += jnp.dot(vt_ref[...], a_ref[...], preferred_element_type=jnp.float32)


def peer_dense(xn, u, vt, r2, e2, ell, w, *, tb=512, na=8):
    t, d = xn.shape
    n_heads, _, nk, _ = r2.shape
    e = u.shape[0]
    eb = na * nk
    tb = min(tb, t)
    assert t % tb == 0 and tb % LANES == 0 and e % eb == 0 and e == nk * nk and na % SUBLANES == 0
    nchunk = tb // LANES
    key_spec = pl.BlockSpec((n_heads, nchunk, nk, LANES), lambda i, j: (0, i, 0, 0))
    row_spec = pl.BlockSpec((n_heads, nchunk, na, LANES), lambda i, j: (0, i, j, 0))
    return pl.pallas_call(
        _peer_dense_kernel,
        grid=(t // tb, e // eb),
        in_specs=[
            pl.BlockSpec((tb, d), lambda i, j: (i, 0)),
            pl.BlockSpec((eb, d), lambda i, j: (j, 0)),
            pl.BlockSpec((d, eb), lambda i, j: (0, j)),
            key_spec, key_spec, row_spec, row_spec,
        ],
        out_specs=pl.BlockSpec((d, tb), lambda i, j: (0, i)),
        out_shape=jax.ShapeDtypeStruct((d, t), jnp.float32),
        scratch_shapes=[pltpu.VMEM((eb, tb), jnp.float32), pltpu.VMEM((eb, tb), jnp.bfloat16)],
        compiler_params=_cparams("parallel", "arbitrary"),
        name="peer_dense",
    )(xn, u, vt, r2, e2, ell, w)


def _final_kernel(x_ref, pt_ref, g_ref, o_ref):
    x = x_ref[...] + pt_ref[...].T
    inv = lax.rsqrt(jnp.mean(x * x, axis=-1, keepdims=True) + RMS_EPS)
    o_ref[...] = x * inv * g_ref[...]


def final_norm(x1, peer_t, gain, *, tm=256):
    t, d = x1.shape
    tm = min(tm, t)
    assert t % tm == 0
    return pl.pallas_call(
        _final_kernel,
        grid=(t // tm,),
        in_specs=[
            pl.BlockSpec((tm, d), lambda i: (i, 0)),
            pl.BlockSpec((d, tm), lambda i: (0, i)),
            pl.BlockSpec((1, d), lambda i: (0, 0)),
        ],
        out_specs=pl.BlockSpec((tm, d), lambda i: (i, 0)),
        out_shape=jax.ShapeDtypeStruct((t, d), jnp.float32),
        compiler_params=_cparams("parallel"),
        name="final_norm",
    )(x1, peer_t, gain.reshape(1, d))


def _layer(x, norm_mix, w_in, conv_w, w_conv_out, w_attn_out, w_out, norm_ffn,
           peer_wq, peer_subkeys, peer_u, peer_v):
    t, d = x.shape
    c = conv_w.shape[1]
    a = w_attn_out.shape[0]
    n_keys, half = peer_subkeys.shape[1:]
    peer_heads = peer_wq.shape[1] // (2 * half)
    dh = LANES // 2
    attn_heads = a // dh
    bf = jnp.bfloat16

    w_in_b = w_in.astype(bf)
    pc = rms_matmul(x, norm_mix, w_in_b, 0, 3 * c, jnp.float32)
    q = rms_matmul(x, norm_mix, w_in_b, 3 * c, a, bf, scale=dh ** -0.5)
    kv = rms_matmul(x, norm_mix, w_in_b, 3 * c + a, 2 * a, bf)
    gates = rms_matmul(x, norm_mix, w_in_b, 3 * c + 3 * a, 2 * d, jnp.float32)

    yc = conv_mixer(pc, conv_w)
    ya = attention(q, kv, attn_heads, dh)
    merged = merge(yc, ya, w_conv_out.astype(bf), w_attn_out.astype(bf), gates)
    x1 = out_proj(merged, w_out.astype(bf), x)

    qp, xn2 = rms_matmul(x1, norm_ffn, peer_wq.astype(bf), 0, peer_wq.shape[1], jnp.float32,
                         emit_xn=True)
    scores = peer_scores(qp, peer_subkeys.astype(bf), peer_heads)
    r2, e2, ell, w = peer_topk(scores, peer_heads)
    peer_t = peer_dense(xn2, peer_u.astype(bf), peer_v.T.astype(bf), r2, e2, ell, w)
    return x1, peer_t


def kernel(x, norm_mix, w_in, conv_w, w_conv_out, w_attn_out, w_out, norm_ffn, peer_wq,
           peer_subkeys, peer_u, peer_v, norm_final):
    b, t, d = x.shape
    depth = norm_mix.shape[0]
    outs = []
    for bi in range(b):
        xb = x[bi]
        peer_t = None
        for layer in range(depth):
            if peer_t is not None:
                xb = xb + peer_t.T
            xb, peer_t = _layer(xb, norm_mix[layer], w_in[layer], conv_w[layer], w_conv_out[layer],
                                w_attn_out[layer], w_out[layer], norm_ffn[layer], peer_wq[layer],
                                peer_subkeys[layer], peer_u[layer], peer_v[layer])
        outs.append(final_norm(xb, peer_t, norm_final))
    return jnp.stack(outs, axis=0)
```

```python
import functools

import jax
import jax.numpy as jnp
from jax import lax
from jax.experimental import pallas as pl
from jax.experimental.pallas import tpu as pltpu

RMS_EPS = 1e-6
LANES = 128
SUBLANES = 8
BF16_ROWS = 16
VMEM_LIMIT = 56 * 1024 * 1024
PEER_TOPK = 16
NEG_INF = float("-inf")
INV_SQRT2 = 0.7071067811865476
NT_DIMS = (((1,), (1,)), ((), ()))


def _cparams(*sem):
    return pltpu.CompilerParams(dimension_semantics=sem, vmem_limit_bytes=VMEM_LIMIT)


def _rmsnorm_kernel(x_ref, g_ref, o_ref):
    x = x_ref[...]
    inv = lax.rsqrt(jnp.mean(x * x, axis=-1, keepdims=True) + RMS_EPS)
    o_ref[...] = (x * inv * g_ref[...]).astype(o_ref.dtype)


def rmsnorm(x, gain, out_dtype, *, tm=512):
    t, d = x.shape
    tm = min(tm, t)
    assert t % tm == 0
    return pl.pallas_call(
        _rmsnorm_kernel,
        grid=(t // tm,),
        in_specs=[pl.BlockSpec((tm, d), lambda i: (i, 0)), pl.BlockSpec((1, d), lambda i: (0, 0))],
        out_specs=pl.BlockSpec((tm, d), lambda i: (i, 0)),
        out_shape=jax.ShapeDtypeStruct((t, d), out_dtype),
        compiler_params=_cparams("parallel"),
        name="rmsnorm",
    )(x, gain.reshape(1, d))


def _matmul_kernel(x_ref, w_ref, o_ref, *, scale):
    acc = jnp.dot(x_ref[...], w_ref[...], preferred_element_type=jnp.float32)
    if scale != 1.0:
        acc = acc * scale
    o_ref[...] = acc.astype(o_ref.dtype)


def matmul(x, w, col0, ncols, out_dtype, *, scale=1.0, tm=1024, tn=1024):
    t, k = x.shape
    tm = min(tm, t)
    tn = min(tn, ncols)
    assert t % tm == 0 and ncols % tn == 0 and col0 % tn == 0
    jb0 = col0 // tn
    return pl.pallas_call(
        functools.partial(_matmul_kernel, scale=scale),
        grid=(t // tm, ncols // tn),
        in_specs=[
            pl.BlockSpec((tm, k), lambda i, j: (i, 0)),
            pl.BlockSpec((k, tn), lambda i, j: (0, jb0 + j)),
        ],
        out_specs=pl.BlockSpec((tm, tn), lambda i, j: (i, j)),
        out_shape=jax.ShapeDtypeStruct((t, ncols), out_dtype),
        compiler_params=_cparams("parallel", "parallel"),
        name="matmul",
    )(x, w)


def _matmul_nt_kernel(wt_ref, x_ref, o_ref):
    o_ref[0] = lax.dot_general(wt_ref[...], x_ref[...], NT_DIMS,
                               preferred_element_type=jnp.float32).astype(o_ref.dtype)


def matmul_nt(wt, x, out_dtype, *, tm, tn=1024):
    t, k = x.shape
    n = wt.shape[0]
    tn = min(tn, n)
    assert t % tm == 0 and n % tn == 0
    return pl.pallas_call(
        _matmul_nt_kernel,
        grid=(t // tm, n // tn),
        in_specs=[
            pl.BlockSpec((tn, k), lambda i, j: (j, 0)),
            pl.BlockSpec((tm, k), lambda i, j: (i, 0)),
        ],
        out_specs=pl.BlockSpec((1, tn, tm), lambda i, j: (i, j, 0)),
        out_shape=jax.ShapeDtypeStruct((t // tm, n, tm), out_dtype),
        compiler_params=_cparams("parallel", "parallel"),
        name="matmul_nt",
    )(wt, x)


def _conv_kernel(cb_ref, cc_ref, ch_ref, cch_ref, chh_ref, w_ref, o_ref, z_ref):
    tm = cb_ref.shape[0]
    z_halo = cch_ref[...] * chh_ref[...]
    z_ref[0:SUBLANES, :] = jnp.where(pl.program_id(0) == 0, 0.0, z_halo)
    z = cc_ref[...] * ch_ref[...]
    z_ref[SUBLANES:, :] = z
    w = w_ref[...]
    k = w.shape[0]
    y = z * w[k - 1:k, :]
    for j in range(1, k):
        y = y + z_ref[pl.ds(SUBLANES - j, tm), :] * w[k - 1 - j:k - j, :]
    o_ref[...] = (cb_ref[...] * y).astype(o_ref.dtype)


def conv_mixer(pc, conv_w, *, tm=512):
    t, c3 = pc.shape
    c = c3 // 3
    k = conv_w.shape[0]
    assert k - 1 <= SUBLANES
    tm = min(tm, t)
    assert t % tm == 0 and tm % SUBLANES == 0
    hb = tm // SUBLANES

    def halo_map(col):
        return lambda i: (jnp.maximum(i * hb - 1, 0), col)

    return pl.pallas_call(
        _conv_kernel,
        grid=(t // tm,),
        in_specs=[
            pl.BlockSpec((tm, c), lambda i: (i, 0)),
            pl.BlockSpec((tm, c), lambda i: (i, 1)),
            pl.BlockSpec((tm, c), lambda i: (i, 2)),
            pl.BlockSpec((SUBLANES, c), halo_map(1)),
            pl.BlockSpec((SUBLANES, c), halo_map(2)),
            pl.BlockSpec((k, c), lambda i: (0, 0)),
        ],
        out_specs=pl.BlockSpec((tm, c), lambda i: (i, 0)),
        out_shape=jax.ShapeDtypeStruct((t, c), jnp.bfloat16),
        scratch_shapes=[pltpu.VMEM((tm + SUBLANES, c), jnp.float32)],
        compiler_params=_cparams("parallel"),
        name="conv_mixer",
    )(pc, pc, pc, pc, pc, conv_w)


ATT_SUB = 128


def _attn_kernel(q_ref, k_ref, vt_ref, tri_ref, o_ref, acc_ref, *, dh):
    blk = q_ref.shape[0]
    nsub = blk // ATT_SUB
    i = pl.program_id(1)
    lane = lax.broadcasted_iota(jnp.int32, (1, LANES), 1)
    head0 = lane < dh
    q2 = q_ref[...]
    zero = jnp.zeros_like(q2)
    qh = (jnp.where(head0, q2, zero), jnp.where(head0, zero, q2))
    tri = tri_ref[...]
    acc_ref[...] = jnp.zeros_like(acc_ref)

    def block(jb, cars, diagonal):
        kb = k_ref[pl.ds(pl.multiple_of(jb * blk, blk), blk), :]
        vt = vt_ref[jb]
        new_cars = []
        for h in range(2):
            zt = lax.dot_general(kb, qh[h], NT_DIMS, preferred_element_type=jnp.float32)
            car = cars[h]
            ws = [None] * nsub
            for s in range(nsub - 1, -1, -1):
                z = zt[s * ATT_SUB:(s + 1) * ATT_SUB]
                nz = -z
                lk = jnp.minimum(nz, 0.0) - jnp.log(1.0 + jnp.exp(jnp.minimum(z, nz)))
                if diagonal:
                    key = lax.broadcasted_iota(jnp.int32, z.shape, 0) + s * ATT_SUB
                    qry = lax.broadcasted_iota(jnp.int32, z.shape, 1)
                    mask = key < qry
                    lk = jnp.where(mask, lk, 0.0)
                hi = lk.astype(jnp.bfloat16)
                lo = (lk - hi.astype(jnp.float32)).astype(jnp.bfloat16)
                r = jnp.dot(tri, jnp.concatenate([hi, lo], axis=0), preferred_element_type=jnp.float32)
                sfx = r[:ATT_SUB]
                tot = r[ATT_SUB:ATT_SUB + 1]
                w = jnp.exp(z + lk + sfx + car)
                if diagonal:
                    w = jnp.where(mask, w, 0.0)
                ws[s] = w.astype(jnp.bfloat16)
                car = car + tot
            wt = jnp.concatenate(ws, axis=0)
            acc_ref[h * dh:(h + 1) * dh, :] += jnp.dot(vt[h * dh:(h + 1) * dh, :], wt,
                                                       preferred_element_type=jnp.float32)
            new_cars.append(car)
        return tuple(new_cars)

    car0 = jnp.zeros((1, blk), jnp.float32)
    cars = block(i, (car0, car0), True)

    def body(n, cars):
        return block(i - 1 - n, cars, False)

    lax.fori_loop(0, i, body, cars)
    o_ref[...] = acc_ref[...].T.astype(o_ref.dtype)


def _tri_matrix():
    s = jnp.arange(ATT_SUB)[:, None]
    j = jnp.arange(ATT_SUB)[None, :]
    later = (j > s).astype(jnp.bfloat16)
    top = jnp.concatenate([later, later], axis=1)
    ones = jnp.ones((BF16_ROWS, 2 * ATT_SUB), jnp.bfloat16)
    return jnp.concatenate([top, ones], axis=0)


def attention(q, k, vt, n_heads, dh, *, blk=512):
    t = q.shape[0]
    assert 2 * dh == LANES and n_heads % 2 == 0 and dh % BF16_ROWS == 0
    assert t % blk == 0 and blk % ATT_SUB == 0 and vt.shape == (t // blk, n_heads * dh, blk)
    npair = n_heads // 2
    nb = t // blk
    return pl.pallas_call(
        functools.partial(_attn_kernel, dh=dh),
        grid=(npair, nb),
        in_specs=[
            pl.BlockSpec((blk, LANES), lambda p, i: (i, p)),
            pl.BlockSpec((t, LANES), lambda p, i: (0, p)),
            pl.BlockSpec((nb, LANES, blk), lambda p, i: (0, p, 0)),
            pl.BlockSpec((ATT_SUB + BF16_ROWS, 2 * ATT_SUB), lambda p, i: (0, 0)),
        ],
        out_specs=pl.BlockSpec((blk, LANES), lambda p, i: (i, p)),
        out_shape=jax.ShapeDtypeStruct((t, n_heads * dh), jnp.bfloat16),
        scratch_shapes=[pltpu.VMEM((LANES, blk), jnp.float32)],
        compiler_params=_cparams("parallel", "parallel"),
        name="stickbreak_attention",
    )(q, k, vt, _tri_matrix())


def _merge_kernel(yc_ref, ya_ref, wc_ref, wa_ref, gc_ref, ga_ref, o_ref):
    yc = jnp.dot(yc_ref[...], wc_ref[...], preferred_element_type=jnp.float32)
    ya = jnp.dot(ya_ref[...], wa_ref[...], preferred_element_type=jnp.float32)
    o_ref[...] = (jax.nn.sigmoid(gc_ref[...]) * yc + jax.nn.sigmoid(ga_ref[...]) * ya).astype(o_ref.dtype)


def merge(yc, ya, wc, wa, gates, *, tm=512, tn=512):
    t, c = yc.shape
    a = ya.shape[1]
    d = wc.shape[1]
    tm = min(tm, t)
    tn = min(tn, d)
    assert t % tm == 0 and d % tn == 0
    nj = d // tn
    return pl.pallas_call(
        _merge_kernel,
        grid=(t // tm, nj),
        in_specs=[
            pl.BlockSpec((tm, c), lambda i, j: (i, 0)),
            pl.BlockSpec((tm, a), lambda i, j: (i, 0)),
            pl.BlockSpec((c, tn), lambda i, j: (0, j)),
            pl.BlockSpec((a, tn), lambda i, j: (0, j)),
            pl.BlockSpec((tm, tn), lambda i, j: (i, j)),
            pl.BlockSpec((tm, tn), lambda i, j: (i, nj + j)),
        ],
        out_specs=pl.BlockSpec((tm, tn), lambda i, j: (i, j)),
        out_shape=jax.ShapeDtypeStruct((t, d), jnp.bfloat16),
        compiler_params=_cparams("parallel", "parallel"),
        name="merge_mixers",
    )(yc, ya, wc, wa, gates, gates)


def _out_proj_kernel(m_ref, w_ref, x_ref, o_ref):
    o_ref[...] = x_ref[...] + jnp.dot(m_ref[...], w_ref[...], preferred_element_type=jnp.float32)


def out_proj(merged, w, x, *, tm=512, tn=512):
    t, d = merged.shape
    n = w.shape[1]
    tm = min(tm, t)
    tn = min(tn, n)
    assert t % tm == 0 and n % tn == 0
    return pl.pallas_call(
        _out_proj_kernel,
        grid=(t // tm, n // tn),
        in_specs=[
            pl.BlockSpec((tm, d), lambda i, j: (i, 0)),
            pl.BlockSpec((d, tn), lambda i, j: (0, j)),
            pl.BlockSpec((tm, tn), lambda i, j: (i, j)),
        ],
        out_specs=pl.BlockSpec((tm, tn), lambda i, j: (i, j)),
        out_shape=jax.ShapeDtypeStruct((t, n), jnp.float32),
        compiler_params=_cparams("parallel", "parallel"),
        name="out_proj",
    )(merged, w, x)


def _scores_kernel(q_ref, k_ref, o_ref):
    half = k_ref.shape[2]
    for hs in range(o_ref.shape[0]):
        q = q_ref[:, hs * half:(hs + 1) * half].astype(jnp.bfloat16)
        keys = k_ref[hs % 2]
        o_ref[hs] = lax.dot_general(keys, q, NT_DIMS, preferred_element_type=jnp.float32)


def peer_scores(qp, subkeys, n_heads, *, tm=512):
    t, qd = qp.shape
    _, nk, half = subkeys.shape
    assert qd == n_heads * 2 * half
    tm = min(tm, t)
    assert t % tm == 0
    return pl.pallas_call(
        _scores_kernel,
        grid=(t // tm,),
        in_specs=[
            pl.BlockSpec((tm, qd), lambda i: (i, 0)),
            pl.BlockSpec((2, nk, half), lambda i: (0, 0, 0)),
        ],
        out_specs=pl.BlockSpec((2 * n_heads, nk, tm), lambda i: (0, 0, i)),
        out_shape=jax.ShapeDtypeStruct((2 * n_heads, nk, t), jnp.float32),
        compiler_params=_cparams("parallel"),
        name="peer_scores",
    )(qp, subkeys)


def _extract_top(cur_ref, rank_ref, count):
    rows, width = cur_ref.shape
    row = lax.broadcasted_iota(jnp.int32, (rows, LANES), 0).astype(jnp.float32)
    slot = lax.broadcasted_iota(jnp.int32, (count, LANES), 0).astype(jnp.float32)
    rank_ref[...] = jnp.full(rank_ref.shape, float(count), jnp.float32)

    def body(r, vals):
        rf = r.astype(jnp.float32)
        new_vals = []
        for tile in range(width // LANES):
            sl = slice(tile * LANES, (tile + 1) * LANES)
            cur = cur_ref[:, sl]
            m = jnp.max(cur, axis=0, keepdims=True)
            first = jnp.min(jnp.where(cur == m, row, float(rows)), axis=0, keepdims=True)
            hit = row == first
            cur_ref[:, sl] = jnp.where(hit, NEG_INF, cur)
            rank_ref[:, sl] = jnp.where(hit, rf, rank_ref[:, sl])
            new_vals.append(jnp.where(slot == rf, m, vals[:, sl]))
        return jnp.concatenate(new_vals, axis=1)

    return lax.fori_loop(0, count, body, jnp.zeros((count, width), jnp.float32))


def _topk_kernel(s_ref, r2_ref, e2_ref, ell_ref, w_ref, cur_ref, rank_ref, cand_ref, crank_ref):
    k = PEER_TOPK
    nchunk = r2_ref.shape[1]
    tl = nchunk * LANES
    cur_ref[:, :tl] = s_ref[0]
    cur_ref[:, tl:] = s_ref[1]
    vals = _extract_top(cur_ref, rank_ref, k)
    a = vals[:, :tl]
    b = vals[:, tl:]
    sub = lax.broadcasted_iota(jnp.int32, (SUBLANES, tl), 0)
    cand_ref[0:SUBLANES] = a[0:1] + b[0:SUBLANES]
    cand_ref[SUBLANES:2 * SUBLANES] = a[0:1] + b[SUBLANES:2 * SUBLANES]
    for i in range(1, SUBLANES):
        cand_ref[(i + 1) * SUBLANES:(i + 2) * SUBLANES] = jnp.where(
            sub < k // (i + 1), a[i:i + 1] + b[0:SUBLANES], NEG_INF)
    base = (SUBLANES + 1) * SUBLANES
    cand_ref[base:base + SUBLANES] = a[SUBLANES:] + b[0:1]
    cvals = _extract_top(cand_ref, crank_ref, k)
    taken = jnp.where(crank_ref[...] < float(k), 1.0, 0.0)
    z = jnp.sum(jnp.exp(cvals - cvals[0:1]), axis=0, keepdims=True)
    rank1 = rank_ref[:, :tl]
    ell = jnp.where(rank1 == 0.0, jnp.sum(taken[0:2 * SUBLANES], axis=0, keepdims=True), 0.0)
    for i in range(1, SUBLANES):
        lim = jnp.sum(taken[(i + 1) * SUBLANES:(i + 2) * SUBLANES], axis=0, keepdims=True)
        ell = jnp.where(rank1 == float(i), lim, ell)
    for j in range(SUBLANES):
        ell = jnp.where(rank1 == float(SUBLANES + j), taken[base + j:base + j + 1], ell)
    e2 = jnp.exp(s_ref[1] - b[0:1])
    w = jnp.exp(s_ref[0] - a[0:1]) * (0.5 / z)
    for c in range(nchunk):
        sl = slice(c * LANES, (c + 1) * LANES)
        r2_ref[0, c] = rank_ref[:, tl + c * LANES:tl + (c + 1) * LANES]
        e2_ref[0, c] = e2[:, sl]
        ell_ref[0, c] = ell[:, sl]
        w_ref[0, c] = w[:, sl]


def peer_topk(scores, n_heads, *, tl=512):
    hs, nk, t = scores.shape
    assert hs == 2 * n_heads and PEER_TOPK == 2 * SUBLANES
    tl = min(tl, t)
    assert t % tl == 0 and tl % LANES == 0
    nchunk = tl // LANES
    ncand = (SUBLANES + 2) * SUBLANES
    out = jax.ShapeDtypeStruct((n_heads, t // LANES, nk, LANES), jnp.float32)
    ospec = pl.BlockSpec((1, nchunk, nk, LANES), lambda h, i: (h, i, 0, 0))
    return pl.pallas_call(
        _topk_kernel,
        grid=(n_heads, t // tl),
        in_specs=[pl.BlockSpec((2, nk, tl), lambda h, i: (h, 0, i))],
        out_specs=[ospec] * 4,
        out_shape=[out] * 4,
        scratch_shapes=[pltpu.VMEM((nk, 2 * tl), jnp.float32), pltpu.VMEM((nk, 2 * tl), jnp.float32),
                        pltpu.VMEM((ncand, tl), jnp.float32), pltpu.VMEM((ncand, tl), jnp.float32)],
        compiler_params=_cparams("parallel", "parallel"),
        name="peer_topk",
    )(scores)


def _peer_dense_kernel(xn_ref, u_ref, vt_ref, r2_ref, e2_ref, ell_ref, w_ref, o_ref, h_ref, a_ref):
    n_heads, nchunk, nk, _ = r2_ref.shape
    na = ell_ref.shape[2]

    @pl.when(pl.program_id(1) == 0)
    def _():
        o_ref[...] = jnp.zeros_like(o_ref)

    h_ref[...] = lax.dot_general(u_ref[...], xn_ref[...], NT_DIMS,
                                 preferred_element_type=jnp.float32)
    for ai in range(na):
        for c in range(nchunk):
            g = jnp.zeros((nk, LANES), jnp.float32)
            for h in range(n_heads):
                lim = ell_ref[h, c, ai:ai + 1, :]
                wa = w_ref[h, c, ai:ai + 1, :]
                g = g + jnp.where(r2_ref[h, c] < lim, e2_ref[h, c], 0.0) * wa
            hid = h_ref[ai * nk:(ai + 1) * nk, c * LANES:(c + 1) * LANES]
            act = hid * (1.0 + lax.erf(hid * INV_SQRT2))
            a_ref[ai * nk:(ai + 1) * nk, c * LANES:(c + 1) * LANES] = (act * g).astype(jnp.bfloat16)
    o_ref[...] += jnp.dot(vt_ref[...], a_ref[...], preferred_element_type=jnp.float32)


def peer_dense(xn, u, vt, r2, e2, ell, w, *, tb=512, na=8):
    t, d = xn.shape
    n_heads, _, nk, _ = r2.shape
    e = u.shape[0]
    eb = na * nk
    tb = min(tb, t)
    assert t % tb == 0 and tb % LANES == 0 and e % eb == 0 and e == nk * nk and na % SUBLANES == 0
    nchunk = tb // LANES
    key_spec = pl.BlockSpec((n_heads, nchunk, nk, LANES), lambda i, j: (0, i, 0, 0))
    row_spec = pl.BlockSpec((n_heads, nchunk, na, LANES), lambda i, j: (0, i, j, 0))
    return pl.pallas_call(
        _peer_dense_kernel,
        grid=(t // tb, e // eb),
        in_specs=[
            pl.BlockSpec((tb, d), lambda i, j: (i, 0)),
            pl.BlockSpec((eb, d), lambda i, j: (j, 0)),
            pl.BlockSpec((d, eb), lambda i, j: (0, j)),
            key_spec, key_spec, row_spec, row_spec,
        ],
        out_specs=pl.BlockSpec((d, tb), lambda i, j: (0, i)),
        out_shape=jax.ShapeDtypeStruct((d, t), jnp.float32),
        scratch_shapes=[pltpu.VMEM((eb, tb), jnp.float32), pltpu.VMEM((eb, tb), jnp.bfloat16)],
        compiler_params=_cparams("parallel", "arbitrary"),
        name="peer_dense",
    )(xn, u, vt, r2, e2, ell, w)


def _final_kernel(x_ref, pt_ref, g_ref, o_ref):
    x = x_ref[...] + pt_ref[...].T
    inv = lax.rsqrt(jnp.mean(x * x, axis=-1, keepdims=True) + RMS_EPS)
    o_ref[...] = x * inv * g_ref[...]


def final_norm(x1, peer_t, gain, *, tm=256):
    t, d = x1.shape
    tm = min(tm, t)
    assert t % tm == 0
    return pl.pallas_call(
        _final_kernel,
        grid=(t // tm,),
        in_specs=[
            pl.BlockSpec((tm, d), lambda i: (i, 0)),
            pl.BlockSpec((d, tm), lambda i: (0, i)),
            pl.BlockSpec((1, d), lambda i: (0, 0)),
        ],
        out_specs=pl.BlockSpec((tm, d), lambda i: (i, 0)),
        out_shape=jax.ShapeDtypeStruct((t, d), jnp.float32),
        compiler_params=_cparams("parallel"),
        name="final_norm",
    )(x1, peer_t, gain.reshape(1, d))


ATT_BLK = 512


def _layer(x, norm_mix, w_in, conv_w, w_conv_out, w_attn_out, w_out, norm_ffn,
           peer_wq, peer_subkeys, peer_u, peer_v):
    t, d = x.shape
    c = conv_w.shape[1]
    a = w_attn_out.shape[0]
    n_keys, half = peer_subkeys.shape[1:]
    peer_heads = peer_wq.shape[1] // (2 * half)
    dh = LANES // 2
    attn_heads = a // dh
    bf = jnp.bfloat16

    w_in_b = w_in.astype(bf)
    xn = rmsnorm(x, norm_mix, bf)
    pc = matmul(xn, w_in_b, 0, 3 * c, jnp.float32)
    q = matmul(xn, w_in_b, 3 * c, a, bf, scale=dh ** -0.5)
    k = matmul(xn, w_in_b, 3 * c + a, a, bf)
    vt = matmul_nt(w_in[:, 3 * c + 2 * a:3 * c + 3 * a].T.astype(bf), xn, bf, tm=min(ATT_BLK, t))
    gates = matmul(xn, w_in_b, 3 * c + 3 * a, 2 * d, jnp.float32)

    yc = conv_mixer(pc, conv_w)
    ya = attention(q, k, vt, attn_heads, dh, blk=min(ATT_BLK, t))
    merged = merge(yc, ya, w_conv_out.astype(bf), w_attn_out.astype(bf), gates)
    x1 = out_proj(merged, w_out.astype(bf), x)

    xn2 = rmsnorm(x1, norm_ffn, bf)
    qp = matmul(xn2, peer_wq.astype(bf), 0, peer_wq.shape[1], jnp.float32)
    scores = peer_scores(qp, peer_subkeys.astype(bf), peer_heads)
    r2, e2, ell, w = peer_topk(scores, peer_heads)
    peer_t = peer_dense(xn2, peer_u.astype(bf), peer_v.T.astype(bf), r2, e2, ell, w)
    return x1, peer_t


def kernel(x, norm_mix, w_in, conv_w, w_conv_out, w_attn_out, w_out, norm_ffn, peer_wq,
           peer_subkeys, peer_u, peer_v, norm_final):
    b, t, d = x.shape
    depth = norm_mix.shape[0]
    outs = []
    for bi in range(b):
        xb = x[bi]
        peer_t = None
        for layer in range(depth):
            if peer_t is not None:
                xb = xb + peer_t.T
            xb, peer_t = _layer(xb, norm_mix[layer], w_in[layer], conv_w[layer], w_conv_out[layer],
                                w_attn_out[layer], w_out[layer], norm_ffn[layer], peer_wq[layer],
                                peer_subkeys[layer], peer_u[layer], peer_v[layer])
        outs.append(final_norm(xb, peer_t, norm_final))
    return jnp.stack(outs, axis=0)
```

```python
import functools

import jax
import jax.numpy as jnp
from jax import lax
from jax.experimental import pallas as pl
from jax.experimental.pallas import tpu as pltpu

RMS_EPS = 1e-6
LANES = 128
SUBLANES = 8
BF16_ROWS = 16
VMEM_LIMIT = 56 * 1024 * 1024
PEER_TOPK = 16
NEG_INF = float("-inf")
INV_SQRT2 = 0.7071067811865476
LOG2E = 1.4426950408889634
NT_DIMS = (((1,), (1,)), ((), ()))


def _cparams(*sem):
    return pltpu.CompilerParams(dimension_semantics=sem, vmem_limit_bytes=VMEM_LIMIT)


def _rmsnorm_kernel(x_ref, g_ref, o_ref):
    x = x_ref[...]
    inv = lax.rsqrt(jnp.mean(x * x, axis=-1, keepdims=True) + RMS_EPS)
    o_ref[...] = (x * inv * g_ref[...]).astype(o_ref.dtype)


def rmsnorm(x, gain, out_dtype, *, tm=512):
    t, d = x.shape
    tm = min(tm, t)
    assert t % tm == 0
    return pl.pallas_call(
        _rmsnorm_kernel,
        grid=(t // tm,),
        in_specs=[pl.BlockSpec((tm, d), lambda i: (i, 0)), pl.BlockSpec((1, d), lambda i: (0, 0))],
        out_specs=pl.BlockSpec((tm, d), lambda i: (i, 0)),
        out_shape=jax.ShapeDtypeStruct((t, d), out_dtype),
        compiler_params=_cparams("parallel"),
        name="rmsnorm",
    )(x, gain.reshape(1, d))


def _matmul_kernel(x_ref, w_ref, o_ref, *, scale):
    acc = jnp.dot(x_ref[...], w_ref[...], preferred_element_type=jnp.float32)
    if scale != 1.0:
        acc = acc * scale
    o_ref[...] = acc.astype(o_ref.dtype)


def matmul(x, w, col0, ncols, out_dtype, *, scale=1.0, tm=1024, tn=1024):
    t, k = x.shape
    tm = min(tm, t)
    tn = min(tn, ncols)
    assert t % tm == 0 and ncols % tn == 0 and col0 % tn == 0
    jb0 = col0 // tn
    return pl.pallas_call(
        functools.partial(_matmul_kernel, scale=scale),
        grid=(t // tm, ncols // tn),
        in_specs=[
            pl.BlockSpec((tm, k), lambda i, j: (i, 0)),
            pl.BlockSpec((k, tn), lambda i, j: (0, jb0 + j)),
        ],
        out_specs=pl.BlockSpec((tm, tn), lambda i, j: (i, j)),
        out_shape=jax.ShapeDtypeStruct((t, ncols), out_dtype),
        compiler_params=_cparams("parallel", "parallel"),
        name="matmul",
    )(x, w)


def _matmul_nt_kernel(wt_ref, x_ref, o_ref):
    o_ref[0] = lax.dot_general(wt_ref[...], x_ref[...], NT_DIMS,
                               preferred_element_type=jnp.float32).astype(o_ref.dtype)


def matmul_nt(wt, x, out_dtype, *, tm, tn=1024):
    t, k = x.shape
    n = wt.shape[0]
    tn = min(tn, n)
    assert t % tm == 0 and n % tn == 0
    return pl.pallas_call(
        _matmul_nt_kernel,
        grid=(t // tm, n // tn),
        in_specs=[
            pl.BlockSpec((tn, k), lambda i, j: (j, 0)),
            pl.BlockSpec((tm, k), lambda i, j: (i, 0)),
        ],
        out_specs=pl.BlockSpec((1, tn, tm), lambda i, j: (i, j, 0)),
        out_shape=jax.ShapeDtypeStruct((t // tm, n, tm), out_dtype),
        compiler_params=_cparams("parallel", "parallel"),
        name="matmul_nt",
    )(wt, x)


def _conv_kernel(cb_ref, cc_ref, ch_ref, cch_ref, chh_ref, w_ref, o_ref, z_ref):
    tm = cb_ref.shape[0]
    z_halo = cch_ref[...] * chh_ref[...]
    z_ref[0:SUBLANES, :] = jnp.where(pl.program_id(0) == 0, 0.0, z_halo)
    z = cc_ref[...] * ch_ref[...]
    z_ref[SUBLANES:, :] = z
    w = w_ref[...]
    k = w.shape[0]
    y = z * w[k - 1:k, :]
    for j in range(1, k):
        y = y + z_ref[pl.ds(SUBLANES - j, tm), :] * w[k - 1 - j:k - j, :]
    o_ref[...] = (cb_ref[...] * y).astype(o_ref.dtype)


def conv_mixer(pc, conv_w, *, tm=512):
    t, c3 = pc.shape
    c = c3 // 3
    k = conv_w.shape[0]
    assert k - 1 <= SUBLANES
    tm = min(tm, t)
    assert t % tm == 0 and tm % SUBLANES == 0
    hb = tm // SUBLANES

    def halo_map(col):
        return lambda i: (jnp.maximum(i * hb - 1, 0), col)

    return pl.pallas_call(
        _conv_kernel,
        grid=(t // tm,),
        in_specs=[
            pl.BlockSpec((tm, c), lambda i: (i, 0)),
            pl.BlockSpec((tm, c), lambda i: (i, 1)),
            pl.BlockSpec((tm, c), lambda i: (i, 2)),
            pl.BlockSpec((SUBLANES, c), halo_map(1)),
            pl.BlockSpec((SUBLANES, c), halo_map(2)),
            pl.BlockSpec((k, c), lambda i: (0, 0)),
        ],
        out_specs=pl.BlockSpec((tm, c), lambda i: (i, 0)),
        out_shape=jax.ShapeDtypeStruct((t, c), jnp.bfloat16),
        scratch_shapes=[pltpu.VMEM((tm + SUBLANES, c), jnp.float32)],
        compiler_params=_cparams("parallel"),
        name="conv_mixer",
    )(pc, pc, pc, pc, pc, conv_w)


ATT_SUB = 128


def _attn_kernel(q_ref, k_ref, vt_ref, tri_ref, o_ref, acc_ref, *, dh):
    blk = q_ref.shape[0]
    nsub = blk // ATT_SUB
    i = pl.program_id(1)
    lane = lax.broadcasted_iota(jnp.int32, (1, LANES), 1)
    head0 = lane < dh
    q2 = q_ref[...]
    zero = jnp.zeros_like(q2)
    qh = (jnp.where(head0, q2, zero), jnp.where(head0, zero, q2))
    tri = tri_ref[...]
    acc_ref[...] = jnp.zeros_like(acc_ref)

    def block(jb, cars, diagonal):
        kb = k_ref[pl.ds(pl.multiple_of(jb * blk, blk), blk), :]
        vt = vt_ref[jb]
        new_cars = []
        zts = [lax.dot_general(kb, qh[h], NT_DIMS, preferred_element_type=jnp.float32)
               for h in range(2)]
        for h in range(2):
            car = cars[h]
            ws = [None] * nsub
            for s in range(nsub - 1, -1, -1):
                z = zts[h][s * ATT_SUB:(s + 1) * ATT_SUB]
                sp = jnp.maximum(z, 0.0) + jnp.log(1.0 + jnp.exp2(jnp.abs(z) * -LOG2E))
                if diagonal:
                    key = lax.broadcasted_iota(jnp.int32, z.shape, 0) + s * ATT_SUB
                    qry = lax.broadcasted_iota(jnp.int32, z.shape, 1)
                    mask = key < qry
                    sp = jnp.where(mask, sp, 0.0)
                hi = sp.astype(jnp.bfloat16)
                lo = (sp - hi.astype(jnp.float32)).astype(jnp.bfloat16)
                r = jnp.dot(tri, jnp.concatenate([hi, lo], axis=0), preferred_element_type=jnp.float32)
                sfx = r[:ATT_SUB]
                tot = r[ATT_SUB:ATT_SUB + 1]
                w = jnp.exp(z - sp - sfx - car)
                if diagonal:
                    w = jnp.where(mask, w, 0.0)
                ws[s] = w.astype(jnp.bfloat16)
                car = car + tot
            wt = jnp.concatenate(ws, axis=0)
            acc_ref[h * dh:(h + 1) * dh, :] += jnp.dot(vt[h * dh:(h + 1) * dh, :], wt,
                                                       preferred_element_type=jnp.float32)
            new_cars.append(car)
        return tuple(new_cars)

    car0 = jnp.zeros((1, blk), jnp.float32)
    cars = block(i, (car0, car0), True)

    def body(n, cars):
        return block(i - 1 - n, cars, False)

    lax.fori_loop(0, i, body, cars)
    o_ref[...] = acc_ref[...].T.astype(o_ref.dtype)


def _tri_matrix():
    s = jnp.arange(ATT_SUB)[:, None]
    j = jnp.arange(ATT_SUB)[None, :]
    later = (j > s).astype(jnp.bfloat16)
    top = jnp.concatenate([later, later], axis=1)
    ones = jnp.ones((BF16_ROWS, 2 * ATT_SUB), jnp.bfloat16)
    return jnp.concatenate([top, ones], axis=0)


def attention(q, k, vt, n_heads, dh, *, blk=512):
    t = q.shape[0]
    assert 2 * dh == LANES and n_heads % 2 == 0 and dh % BF16_ROWS == 0
    assert t % blk == 0 and blk % ATT_SUB == 0 and vt.shape == (t // blk, n_heads * dh, blk)
    npair = n_heads // 2
    nb = t // blk
    return pl.pallas_call(
        functools.partial(_attn_kernel, dh=dh),
        grid=(npair, nb),
        in_specs=[
            pl.BlockSpec((blk, LANES), lambda p, i: (i, p)),
            pl.BlockSpec((t, LANES), lambda p, i: (0, p)),
            pl.BlockSpec((nb, LANES, blk), lambda p, i: (0, p, 0)),
            pl.BlockSpec((ATT_SUB + BF16_ROWS, 2 * ATT_SUB), lambda p, i: (0, 0)),
        ],
        out_specs=pl.BlockSpec((blk, LANES), lambda p, i: (i, p)),
        out_shape=jax.ShapeDtypeStruct((t, n_heads * dh), jnp.bfloat16),
        scratch_shapes=[pltpu.VMEM((LANES, blk), jnp.float32)],
        compiler_params=_cparams("parallel", "parallel"),
        name="stickbreak_attention",
    )(q, k, vt, _tri_matrix())


def _merge_kernel(yc_ref, ya_ref, wc_ref, wa_ref, gc_ref, ga_ref, o_ref):
    yc = jnp.dot(yc_ref[...], wc_ref[...], preferred_element_type=jnp.float32)
    ya = jnp.dot(ya_ref[...], wa_ref[...], preferred_element_type=jnp.float32)
    o_ref[...] = (jax.nn.sigmoid(gc_ref[...]) * yc + jax.nn.sigmoid(ga_ref[...]) * ya).astype(o_ref.dtype)


def merge(yc, ya, wc, wa, gates, *, tm=1024, tn=1024):
    t, c = yc.shape
    a = ya.shape[1]
    d = wc.shape[1]
    tm = min(tm, t)
    tn = min(tn, d)
    assert t % tm == 0 and d % tn == 0
    nj = d // tn
    return pl.pallas_call(
        _merge_kernel,
        grid=(t // tm, nj),
        in_specs=[
            pl.BlockSpec((tm, c), lambda i, j: (i, 0)),
            pl.BlockSpec((tm, a), lambda i, j: (i, 0)),
            pl.BlockSpec((c, tn), lambda i, j: (0, j)),
            pl.BlockSpec((a, tn), lambda i, j: (0, j)),
            pl.BlockSpec((tm, tn), lambda i, j: (i, j)),
            pl.BlockSpec((tm, tn), lambda i, j: (i, nj + j)),
        ],
        out_specs=pl.BlockSpec((tm, tn), lambda i, j: (i, j)),
        out_shape=jax.ShapeDtypeStruct((t, d), jnp.bfloat16),
        compiler_params=_cparams("parallel", "parallel"),
        name="merge_mixers",
    )(yc, ya, wc, wa, gates, gates)


def _out_proj_kernel(m_ref, w_ref, x_ref, o_ref):
    o_ref[...] = x_ref[...] + jnp.dot(m_ref[...], w_ref[...], preferred_element_type=jnp.float32)


def out_proj(merged, w, x, *, tm=1024, tn=1024):
    t, d = merged.shape
    n = w.shape[1]
    tm = min(tm, t)
    tn = min(tn, n)
    assert t % tm == 0 and n % tn == 0
    return pl.pallas_call(
        _out_proj_kernel,
        grid=(t // tm, n // tn),
        in_specs=[
            pl.BlockSpec((tm, d), lambda i, j: (i, 0)),
            pl.BlockSpec((d, tn), lambda i, j: (0, j)),
            pl.BlockSpec((tm, tn), lambda i, j: (i, j)),
        ],
        out_specs=pl.BlockSpec((tm, tn), lambda i, j: (i, j)),
        out_shape=jax.ShapeDtypeStruct((t, n), jnp.float32),
        compiler_params=_cparams("parallel", "parallel"),
        name="out_proj",
    )(merged, w, x)


def _scores_kernel(q_ref, k_ref, o_ref):
    half = k_ref.shape[2]
    for hs in range(o_ref.shape[0]):
        q = q_ref[:, hs * half:(hs + 1) * half].astype(jnp.bfloat16)
        keys = k_ref[hs % 2]
        o_ref[hs] = lax.dot_general(keys, q, NT_DIMS, preferred_element_type=jnp.float32)


def peer_scores(qp, subkeys, n_heads, *, tm=512):
    t, qd = qp.shape
    _, nk, half = subkeys.shape
    assert qd == n_heads * 2 * half
    tm = min(tm, t)
    assert t % tm == 0
    return pl.pallas_call(
        _scores_kernel,
        grid=(t // tm,),
        in_specs=[
            pl.BlockSpec((tm, qd), lambda i: (i, 0)),
            pl.BlockSpec((2, nk, half), lambda i: (0, 0, 0)),
        ],
        out_specs=pl.BlockSpec((2 * n_heads, nk, tm), lambda i: (0, 0, i)),
        out_shape=jax.ShapeDtypeStruct((2 * n_heads, nk, t), jnp.float32),
        compiler_params=_cparams("parallel"),
        name="peer_scores",
    )(qp, subkeys)


def _extract_top(cur_ref, rank_ref, count):
    rows, width = cur_ref.shape
    row = lax.broadcasted_iota(jnp.int32, (rows, LANES), 0).astype(jnp.float32)
    slot = lax.broadcasted_iota(jnp.int32, (count, LANES), 0).astype(jnp.float32)
    rank_ref[...] = jnp.full(rank_ref.shape, float(count), jnp.float32)

    def body(r, vals):
        rf = r.astype(jnp.float32)
        new_vals = []
        for tile in range(width // LANES):
            sl = slice(tile * LANES, (tile + 1) * LANES)
            cur = cur_ref[:, sl]
            m = jnp.max(cur, axis=0, keepdims=True)
            first = jnp.min(jnp.where(cur == m, row, float(rows)), axis=0, keepdims=True)
            hit = row == first
            cur_ref[:, sl] = jnp.where(hit, NEG_INF, cur)
            rank_ref[:, sl] = jnp.where(hit, rf, rank_ref[:, sl])
            new_vals.append(jnp.where(slot == rf, m, vals[:, sl]))
        return jnp.concatenate(new_vals, axis=1)

    return lax.fori_loop(0, count, body, jnp.zeros((count, width), jnp.float32))


def _topk_kernel(s_ref, r2_ref, e2_ref, ell_ref, w_ref, cur_ref, rank_ref, cand_ref, crank_ref):
    k = PEER_TOPK
    nchunk = r2_ref.shape[1]
    tl = nchunk * LANES
    cur_ref[:, :tl] = s_ref[0]
    cur_ref[:, tl:] = s_ref[1]
    vals = _extract_top(cur_ref, rank_ref, k)
    a = vals[:, :tl]
    b = vals[:, tl:]
    sub = lax.broadcasted_iota(jnp.int32, (SUBLANES, tl), 0)
    cand_ref[0:SUBLANES] = a[0:1] + b[0:SUBLANES]
    cand_ref[SUBLANES:2 * SUBLANES] = a[0:1] + b[SUBLANES:2 * SUBLANES]
    for i in range(1, SUBLANES):
        cand_ref[(i + 1) * SUBLANES:(i + 2) * SUBLANES] = jnp.where(
            sub < k // (i + 1), a[i:i + 1] + b[0:SUBLANES], NEG_INF)
    base = (SUBLANES + 1) * SUBLANES
    cand_ref[base:base + SUBLANES] = a[SUBLANES:] + b[0:1]
    cvals = _extract_top(cand_ref, crank_ref, k)
    taken = jnp.where(crank_ref[...] < float(k), 1.0, 0.0)
    z = jnp.sum(jnp.exp(cvals - cvals[0:1]), axis=0, keepdims=True)
    rank1 = rank_ref[:, :tl]
    ell = jnp.where(rank1 == 0.0, jnp.sum(taken[0:2 * SUBLANES], axis=0, keepdims=True), 0.0)
    for i in range(1, SUBLANES):
        lim = jnp.sum(taken[(i + 1) * SUBLANES:(i + 2) * SUBLANES], axis=0, keepdims=True)
        ell = jnp.where(rank1 == float(i), lim, ell)
    for j in range(SUBLANES):
        ell = jnp.where(rank1 == float(SUBLANES + j), taken[base + j:base + j + 1], ell)
    e2 = jnp.exp(s_ref[1] - b[0:1])
    w = jnp.exp(s_ref[0] - a[0:1]) * (0.5 / z)
    for c in range(nchunk):
        sl = slice(c * LANES, (c + 1) * LANES)
        r2_ref[0, c] = rank_ref[:, tl + c * LANES:tl + (c + 1) * LANES].astype(r2_ref.dtype)
        e2_ref[0, c] = e2[:, sl].astype(e2_ref.dtype)
        ell_ref[0, c] = ell[:, sl]
        w_ref[0, c] = w[:, sl]


def peer_topk(scores, n_heads, *, tl=512):
    hs, nk, t = scores.shape
    assert hs == 2 * n_heads and PEER_TOPK == 2 * SUBLANES
    tl = min(tl, t)
    assert t % tl == 0 and tl % LANES == 0
    nchunk = tl // LANES
    ncand = (SUBLANES + 2) * SUBLANES
    shape = (n_heads, t // LANES, nk, LANES)
    ospec = pl.BlockSpec((1, nchunk, nk, LANES), lambda h, i: (h, i, 0, 0))
    return pl.pallas_call(
        _topk_kernel,
        grid=(n_heads, t // tl),
        in_specs=[pl.BlockSpec((2, nk, tl), lambda h, i: (h, 0, i))],
        out_specs=[ospec] * 4,
        out_shape=[jax.ShapeDtypeStruct(shape, jnp.float32)] * 4,
        scratch_shapes=[pltpu.VMEM((nk, 2 * tl), jnp.float32), pltpu.VMEM((nk, 2 * tl), jnp.float32),
                        pltpu.VMEM((ncand, tl), jnp.float32), pltpu.VMEM((ncand, tl), jnp.float32)],
        compiler_params=_cparams("parallel", "parallel"),
        name="peer_topk",
    )(scores)


def _peer_hidden_kernel(u_ref, xn_ref, o_ref):
    o_ref[...] = lax.dot_general(u_ref[...], xn_ref[...], NT_DIMS, preferred_element_type=jnp.float32)


def peer_hidden(xn, u, *, tb=1024, eb=1024):
    t, d = xn.shape
    e = u.shape[0]
    tb = min(tb, t)
    assert t % tb == 0 and e % eb == 0
    return pl.pallas_call(
        _peer_hidden_kernel,
        grid=(t // tb, e // eb),
        in_specs=[
            pl.BlockSpec((eb, d), lambda i, j: (j, 0)),
            pl.BlockSpec((tb, d), lambda i, j: (i, 0)),
        ],
        out_specs=pl.BlockSpec((eb, tb), lambda i, j: (j, i)),
        out_shape=jax.ShapeDtypeStruct((e, t), jnp.float32),
        compiler_params=_cparams("parallel", "parallel"),
        name="peer_hidden",
    )(u, xn)


def _peer_dense_kernel(h_ref, vt_ref, r2_ref, e2_ref, ell_ref, w_ref, o_ref, a_ref):
    n_heads, nchunk, nk, _ = r2_ref.shape
    na = ell_ref.shape[2]

    @pl.when(pl.program_id(1) == 0)
    def _():
        o_ref[...] = jnp.zeros_like(o_ref)

    for ai in range(na):
        for c in range(nchunk):
            g = jnp.zeros((nk, LANES), jnp.float32)
            for h in range(n_heads):
                lim = ell_ref[h, c, ai:ai + 1, :]
                wa = w_ref[h, c, ai:ai + 1, :]
                g = g + jnp.where(r2_ref[h, c] < lim, e2_ref[h, c], 0.0) * wa
            hid = h_ref[ai * nk:(ai + 1) * nk, c * LANES:(c + 1) * LANES]
            act = hid * (1.0 + lax.erf(hid * INV_SQRT2))
            a_ref[ai * nk:(ai + 1) * nk, c * LANES:(c + 1) * LANES] = (act * g).astype(jnp.bfloat16)
    o_ref[...] += jnp.dot(vt_ref[...], a_ref[...], preferred_element_type=jnp.float32)


def peer_dense(hid_t, vt, r2, e2, ell, w, *, tb=512, na=8):
    e, t = hid_t.shape
    d = vt.shape[0]
    n_heads, _, nk, _ = r2.shape
    eb = na * nk
    tb = min(tb, t)
    assert t % tb == 0 and tb % LANES == 0 and e % eb == 0 and e == nk * nk and na % SUBLANES == 0
    nchunk = tb // LANES
    key_spec = pl.BlockSpec((n_heads, nchunk, nk, LANES), lambda i, j: (0, i, 0, 0))
    row_spec = pl.BlockSpec((n_heads, nchunk, na, LANES), lambda i, j: (0, i, j, 0))
    return pl.pallas_call(
        _peer_dense_kernel,
        grid=(t // tb, e // eb),
        in_specs=[
            pl.BlockSpec((eb, tb), lambda i, j: (j, i)),
            pl.BlockSpec((d, eb), lambda i, j: (0, j)),
            key_spec, key_spec, row_spec, row_spec,
        ],
        out_specs=pl.BlockSpec((d, tb), lambda i, j: (0, i)),
        out_shape=jax.ShapeDtypeStruct((d, t), jnp.float32),
        scratch_shapes=[pltpu.VMEM((eb, tb), jnp.bfloat16)],
        compiler_params=_cparams("parallel", "arbitrary"),
        name="peer_dense",
    )(hid_t, vt, r2, e2, ell, w)


def _final_kernel(x_ref, pt_ref, g_ref, o_ref):
    x = x_ref[...] + pt_ref[...].T
    inv = lax.rsqrt(jnp.mean(x * x, axis=-1, keepdims=True) + RMS_EPS)
    o_ref[...] = x * inv * g_ref[...]


def final_norm(x1, peer_t, gain, *, tm=256):
    t, d = x1.shape
    tm = min(tm, t)
    assert t % tm == 0
    return pl.pallas_call(
        _final_kernel,
        grid=(t // tm,),
        in_specs=[
            pl.BlockSpec((tm, d), lambda i: (i, 0)),
            pl.BlockSpec((d, tm), lambda i: (0, i)),
            pl.BlockSpec((1, d), lambda i: (0, 0)),
        ],
        out_specs=pl.BlockSpec((tm, d), lambda i: (i, 0)),
        out_shape=jax.ShapeDtypeStruct((t, d), jnp.float32),
        compiler_params=_cparams("parallel"),
        name="final_norm",
    )(x1, peer_t, gain.reshape(1, d))


ATT_BLK = 512


def _layer(x, norm_mix, w_in, conv_w, w_conv_out, w_attn_out, w_out, norm_ffn,
           peer_wq, peer_subkeys, peer_u, peer_v):
    t, d = x.shape
    c = conv_w.shape[1]
    a = w_attn_out.shape[0]
    n_keys, half = peer_subkeys.shape[1:]
    peer_heads = peer_wq.shape[1] // (2 * half)
    dh = LANES // 2
    attn_heads = a // dh
    bf = jnp.bfloat16

    w_in_b = w_in.astype(bf)
    xn = rmsnorm(x, norm_mix, bf)
    pc = matmul(xn, w_in_b, 0, 3 * c, jnp.float32)
    q = matmul(xn, w_in_b, 3 * c, a, bf, scale=dh ** -0.5)
    k = matmul(xn, w_in_b, 3 * c + a, a, bf)
    vt = matmul_nt(w_in[:, 3 * c + 2 * a:3 * c + 3 * a].T.astype(bf), xn, bf, tm=min(ATT_BLK, t))
    gates = matmul(xn, w_in_b, 3 * c + 3 * a, 2 * d, jnp.float32)

    yc = conv_mixer(pc, conv_w)
    ya = attention(q, k, vt, attn_heads, dh, blk=min(ATT_BLK, t))
    merged = merge(yc, ya, w_conv_out.astype(bf), w_attn_out.astype(bf), gates)
    x1 = out_proj(merged, w_out.astype(bf), x)

    xn2 = rmsnorm(x1, norm_ffn, bf)
    qp = matmul(xn2, peer_wq.astype(bf), 0, peer_wq.shape[1], jnp.float32)
    scores = peer_scores(qp, peer_subkeys.astype(bf), peer_heads)
    r2, e2, ell, w = peer_topk(scores, peer_heads)
    hid_t = peer_hidden(xn2, peer_u.astype(bf))
    peer_t = peer_dense(hid_t, peer_v.T.astype(bf), r2, e2, ell, w)
    return x1, peer_t


def kernel(x, norm_mix, w_in, conv_w, w_conv_out, w_attn_out, w_out, norm_ffn, peer_wq,
           peer_subkeys, peer_u, peer_v, norm_final):
    b, t, d = x.shape
    depth = norm_mix.shape[0]
    outs = []
    for bi in range(b):
        xb = x[bi]
        peer_t = None
        for layer in range(depth):
            if peer_t is not None:
                xb = xb + peer_t.T
            xb, peer_t = _layer(xb, norm_mix[layer], w_in[layer], conv_w[layer], w_conv_out[layer],
                                w_attn_out[layer], w_out[layer], norm_ffn[layer], peer_wq[layer],
                                peer_subkeys[layer], peer_u[layer], peer_v[layer])
        outs.append(final_norm(xb, peer_t, norm_final))
    return jnp.stack(outs, axis=0)
```

```python
import functools

import jax
import jax.numpy as jnp
from jax import lax
from jax.experimental import pallas as pl
from jax.experimental.pallas import tpu as pltpu

RMS_EPS = 1e-6
LANES = 128
SUBLANES = 8
BF16_ROWS = 16
VMEM_LIMIT = 56 * 1024 * 1024
PEER_TOPK = 16
NEG_INF = float("-inf")
INV_SQRT2 = 0.7071067811865476
LOG2E = 1.4426950408889634
NT_DIMS = (((1,), (1,)), ((), ()))


def _cparams(*sem):
    return pltpu.CompilerParams(dimension_semantics=sem, vmem_limit_bytes=VMEM_LIMIT)


def _rmsnorm_kernel(x_ref, g_ref, o_ref):
    x = x_ref[...]
    inv = lax.rsqrt(jnp.mean(x * x, axis=-1, keepdims=True) + RMS_EPS)
    o_ref[...] = (x * inv * g_ref[...]).astype(o_ref.dtype)


def rmsnorm(x, gain, out_dtype, *, tm=512):
    t, d = x.shape
    tm = min(tm, t)
    assert t % tm == 0
    return pl.pallas_call(
        _rmsnorm_kernel,
        grid=(t // tm,),
        in_specs=[pl.BlockSpec((tm, d), lambda i: (i, 0)), pl.BlockSpec((1, d), lambda i: (0, 0))],
        out_specs=pl.BlockSpec((tm, d), lambda i: (i, 0)),
        out_shape=jax.ShapeDtypeStruct((t, d), out_dtype),
        compiler_params=_cparams("parallel"),
        name="rmsnorm",
    )(x, gain.reshape(1, d))


def _matmul_kernel(x_ref, w_ref, o_ref, wb_ref, *, scale):
    @pl.when(pl.program_id(1) == 0)
    def _():
        wb_ref[...] = w_ref[...].astype(jnp.bfloat16)

    acc = jnp.dot(x_ref[...], wb_ref[...], preferred_element_type=jnp.float32)
    if scale != 1.0:
        acc = acc * scale
    o_ref[...] = acc.astype(o_ref.dtype)


def matmul(x, w, col0, ncols, out_dtype, *, scale=1.0, tm=1024, tn=1024):
    t, k = x.shape
    tm = min(tm, t)
    tn = min(tn, ncols)
    assert t % tm == 0 and ncols % tn == 0 and col0 % tn == 0
    jb0 = col0 // tn
    return pl.pallas_call(
        functools.partial(_matmul_kernel, scale=scale),
        grid=(ncols // tn, t // tm),
        in_specs=[
            pl.BlockSpec((tm, k), lambda j, i: (i, 0)),
            pl.BlockSpec((k, tn), lambda j, i: (0, jb0 + j)),
        ],
        out_specs=pl.BlockSpec((tm, tn), lambda j, i: (i, j)),
        out_shape=jax.ShapeDtypeStruct((t, ncols), out_dtype),
        scratch_shapes=[pltpu.VMEM((k, tn), jnp.bfloat16)],
        compiler_params=_cparams("parallel", "arbitrary"),
        name="matmul",
    )(x, w)


def _matmul_nt_kernel(wt_ref, x_ref, o_ref):
    o_ref[0] = lax.dot_general(wt_ref[...], x_ref[...], NT_DIMS,
                               preferred_element_type=jnp.float32).astype(o_ref.dtype)


def matmul_nt(wt, x, out_dtype, *, tm, tn=1024):
    t, k = x.shape
    n = wt.shape[0]
    tn = min(tn, n)
    assert t % tm == 0 and n % tn == 0
    return pl.pallas_call(
        _matmul_nt_kernel,
        grid=(t // tm, n // tn),
        in_specs=[
            pl.BlockSpec((tn, k), lambda i, j: (j, 0)),
            pl.BlockSpec((tm, k), lambda i, j: (i, 0)),
        ],
        out_specs=pl.BlockSpec((1, tn, tm), lambda i, j: (i, j, 0)),
        out_shape=jax.ShapeDtypeStruct((t // tm, n, tm), out_dtype),
        compiler_params=_cparams("parallel", "parallel"),
        name="matmul_nt",
    )(wt, x)


def _conv_kernel(cb_ref, cc_ref, ch_ref, cch_ref, chh_ref, w_ref, o_ref, z_ref):
    tm = cb_ref.shape[0]
    z_halo = cch_ref[...] * chh_ref[...]
    z_ref[0:SUBLANES, :] = jnp.where(pl.program_id(0) == 0, 0.0, z_halo)
    z = cc_ref[...] * ch_ref[...]
    z_ref[SUBLANES:, :] = z
    w = w_ref[...]
    k = w.shape[0]
    y = z * w[k - 1:k, :]
    for j in range(1, k):
        y = y + z_ref[pl.ds(SUBLANES - j, tm), :] * w[k - 1 - j:k - j, :]
    o_ref[...] = (cb_ref[...] * y).astype(o_ref.dtype)


def conv_mixer(pc, conv_w, *, tm=512):
    t, c3 = pc.shape
    c = c3 // 3
    k = conv_w.shape[0]
    assert k - 1 <= SUBLANES
    tm = min(tm, t)
    assert t % tm == 0 and tm % SUBLANES == 0
    hb = tm // SUBLANES

    def halo_map(col):
        return lambda i: (jnp.maximum(i * hb - 1, 0), col)

    return pl.pallas_call(
        _conv_kernel,
        grid=(t // tm,),
        in_specs=[
            pl.BlockSpec((tm, c), lambda i: (i, 0)),
            pl.BlockSpec((tm, c), lambda i: (i, 1)),
            pl.BlockSpec((tm, c), lambda i: (i, 2)),
            pl.BlockSpec((SUBLANES, c), halo_map(1)),
            pl.BlockSpec((SUBLANES, c), halo_map(2)),
            pl.BlockSpec((k, c), lambda i: (0, 0)),
        ],
        out_specs=pl.BlockSpec((tm, c), lambda i: (i, 0)),
        out_shape=jax.ShapeDtypeStruct((t, c), jnp.bfloat16),
        scratch_shapes=[pltpu.VMEM((tm + SUBLANES, c), jnp.float32)],
        compiler_params=_cparams("parallel"),
        name="conv_mixer",
    )(pc, pc, pc, pc, pc, conv_w)


ATT_SUB = 128


def _attn_kernel(q_ref, k_ref, vt_ref, tri_ref, o_ref, acc_ref, *, dh):
    blk = q_ref.shape[0]
    nheads = 2 * (q_ref.shape[1] // LANES)
    nsub = blk // ATT_SUB
    i = pl.program_id(1)
    lane = lax.broadcasted_iota(jnp.int32, (1, LANES), 1)
    head0 = lane < dh
    qh = []
    for g in range(nheads // 2):
        q2 = q_ref[:, g * LANES:(g + 1) * LANES]
        zero = jnp.zeros_like(q2)
        qh += [jnp.where(head0, q2, zero), jnp.where(head0, zero, q2)]
    tri = tri_ref[...]
    acc_ref[...] = jnp.zeros_like(acc_ref)

    def block(jb, cars, diagonal):
        kb = k_ref[pl.ds(pl.multiple_of(jb * blk, blk), blk), :]
        vt = vt_ref[jb]
        new_cars = []
        zts = [lax.dot_general(kb[:, (h // 2) * LANES:(h // 2 + 1) * LANES], qh[h], NT_DIMS,
                               preferred_element_type=jnp.float32)
               for h in range(nheads)]
        for h in range(nheads):
            car = cars[h]
            ws = [None] * nsub
            for s in range(nsub - 1, -1, -1):
                z = zts[h][s * ATT_SUB:(s + 1) * ATT_SUB]
                sp = jnp.maximum(z, 0.0) + jnp.log(1.0 + jnp.exp2(jnp.abs(z) * -LOG2E))
                logsig = z - sp
                if diagonal:
                    key = lax.broadcasted_iota(jnp.int32, z.shape, 0) + s * ATT_SUB
                    qry = lax.broadcasted_iota(jnp.int32, z.shape, 1)
                    mask = key < qry
                    sp = jnp.where(mask, sp, 0.0)
                r = jnp.dot(tri, sp.astype(jnp.bfloat16), preferred_element_type=jnp.float32)
                sfx = r[:ATT_SUB]
                tot = r[ATT_SUB:ATT_SUB + 1]
                w = jnp.exp(logsig - sfx - car)
                if diagonal:
                    w = jnp.where(mask, w, 0.0)
                ws[s] = w.astype(jnp.bfloat16)
                car = car + tot
            wt = jnp.concatenate(ws, axis=0)
            acc_ref[h * dh:(h + 1) * dh, :] += jnp.dot(vt[h * dh:(h + 1) * dh, :], wt,
                                                       preferred_element_type=jnp.float32)
            new_cars.append(car)
        return tuple(new_cars)

    car0 = jnp.zeros((1, blk), jnp.float32)
    cars = block(i, (car0,) * nheads, True)

    def body(n, cars):
        return block(i - 1 - n, cars, False)

    lax.fori_loop(0, i, body, cars)
    o_ref[...] = acc_ref[...].T.astype(o_ref.dtype)


def _tri_matrix():
    s = jnp.arange(ATT_SUB)[:, None]
    j = jnp.arange(ATT_SUB)[None, :]
    later = (j > s).astype(jnp.bfloat16)
    ones = jnp.ones((BF16_ROWS, ATT_SUB), jnp.bfloat16)
    return jnp.concatenate([later, ones], axis=0)


def attention(q, k, vt, n_heads, dh, *, blk=512, heads_per_step=4):
    t = q.shape[0]
    assert 2 * dh == LANES and heads_per_step % 2 == 0 and dh % BF16_ROWS == 0
    assert n_heads % heads_per_step == 0
    assert t % blk == 0 and blk % ATT_SUB == 0 and vt.shape == (t // blk, n_heads * dh, blk)
    ngroup = n_heads // heads_per_step
    gw = heads_per_step * dh
    nb = t // blk
    return pl.pallas_call(
        functools.partial(_attn_kernel, dh=dh),
        grid=(ngroup, nb),
        in_specs=[
            pl.BlockSpec((blk, gw), lambda p, i: (i, p)),
            pl.BlockSpec((t, gw), lambda p, i: (0, p)),
            pl.BlockSpec((nb, gw, blk), lambda p, i: (0, p, 0)),
            pl.BlockSpec((ATT_SUB + BF16_ROWS, ATT_SUB), lambda p, i: (0, 0)),
        ],
        out_specs=pl.BlockSpec((blk, gw), lambda p, i: (i, p)),
        out_shape=jax.ShapeDtypeStruct((t, n_heads * dh), jnp.bfloat16),
        scratch_shapes=[pltpu.VMEM((gw, blk), jnp.float32)],
        compiler_params=_cparams("parallel", "parallel"),
        name="stickbreak_attention",
    )(q, k, vt, _tri_matrix())


def _merge_kernel(yc_ref, ya_ref, wc_ref, wa_ref, gc_ref, ga_ref, o_ref):
    yc = jnp.dot(yc_ref[...], wc_ref[...], preferred_element_type=jnp.float32)
    ya = jnp.dot(ya_ref[...], wa_ref[...], preferred_element_type=jnp.float32)
    o_ref[...] = (jax.nn.sigmoid(gc_ref[...]) * yc + jax.nn.sigmoid(ga_ref[...]) * ya).astype(o_ref.dtype)


def merge(yc, ya, wc, wa, gates, *, tm=1024, tn=1024):
    t, c = yc.shape
    a = ya.shape[1]
    d = wc.shape[1]
    tm = min(tm, t)
    tn = min(tn, d)
    assert t % tm == 0 and d % tn == 0
    nj = d // tn
    return pl.pallas_call(
        _merge_kernel,
        grid=(t // tm, nj),
        in_specs=[
            pl.BlockSpec((tm, c), lambda i, j: (i, 0)),
            pl.BlockSpec((tm, a), lambda i, j: (i, 0)),
            pl.BlockSpec((c, tn), lambda i, j: (0, j)),
            pl.BlockSpec((a, tn), lambda i, j: (0, j)),
            pl.BlockSpec((tm, tn), lambda i, j: (i, j)),
            pl.BlockSpec((tm, tn), lambda i, j: (i, nj + j)),
        ],
        out_specs=pl.BlockSpec((tm, tn), lambda i, j: (i, j)),
        out_shape=jax.ShapeDtypeStruct((t, d), jnp.bfloat16),
        compiler_params=_cparams("parallel", "parallel"),
        name="merge_mixers",
    )(yc, ya, wc, wa, gates, gates)


def _out_proj_kernel(m_ref, w_ref, x_ref, o_ref):
    o_ref[...] = x_ref[...] + jnp.dot(m_ref[...], w_ref[...], preferred_element_type=jnp.float32)


def out_proj(merged, w, x, *, tm=1024, tn=1024):
    t, d = merged.shape
    n = w.shape[1]
    tm = min(tm, t)
    tn = min(tn, n)
    assert t % tm == 0 and n % tn == 0
    return pl.pallas_call(
        _out_proj_kernel,
        grid=(t // tm, n // tn),
        in_specs=[
            pl.BlockSpec((tm, d), lambda i, j: (i, 0)),
            pl.BlockSpec((d, tn), lambda i, j: (0, j)),
            pl.BlockSpec((tm, tn), lambda i, j: (i, j)),
        ],
        out_specs=pl.BlockSpec((tm, tn), lambda i, j: (i, j)),
        out_shape=jax.ShapeDtypeStruct((t, n), jnp.float32),
        compiler_params=_cparams("parallel", "parallel"),
        name="out_proj",
    )(merged, w, x)


def _scores_kernel(q_ref, k_ref, o_ref):
    half = k_ref.shape[2]
    for hs in range(o_ref.shape[0]):
        q = q_ref[:, hs * half:(hs + 1) * half].astype(jnp.bfloat16)
        keys = k_ref[hs % 2]
        o_ref[hs] = lax.dot_general(keys, q, NT_DIMS, preferred_element_type=jnp.float32)


def peer_scores(qp, subkeys, n_heads, *, tm=512):
    t, qd = qp.shape
    _, nk, half = subkeys.shape
    assert qd == n_heads * 2 * half
    tm = min(tm, t)
    assert t % tm == 0
    return pl.pallas_call(
        _scores_kernel,
        grid=(t // tm,),
        in_specs=[
            pl.BlockSpec((tm, qd), lambda i: (i, 0)),
            pl.BlockSpec((2, nk, half), lambda i: (0, 0, 0)),
        ],
        out_specs=pl.BlockSpec((2 * n_heads, nk, tm), lambda i: (0, 0, i)),
        out_shape=jax.ShapeDtypeStruct((2 * n_heads, nk, t), jnp.float32),
        compiler_params=_cparams("parallel"),
        name="peer_scores",
    )(qp, subkeys)


def _extract_top(cur_ref, rank_ref, count):
    rows, width = cur_ref.shape
    row = lax.broadcasted_iota(jnp.int32, (rows, LANES), 0).astype(jnp.float32)
    slot = lax.broadcasted_iota(jnp.int32, (count, LANES), 0).astype(jnp.float32)
    rank_ref[...] = jnp.full(rank_ref.shape, float(count), jnp.float32)

    def body(r, vals):
        rf = lax.convert_element_type(r, jnp.float32)
        new_vals = []
        for tile in range(width // LANES):
            sl = slice(tile * LANES, (tile + 1) * LANES)
            cur = cur_ref[:, sl]
            m = jnp.max(cur, axis=0, keepdims=True)
            first = jnp.min(jnp.where(cur == m, row, float(rows)), axis=0, keepdims=True)
            hit = row == first
            cur_ref[:, sl] = jnp.where(hit, NEG_INF, cur)
            rank_ref[:, sl] = jnp.where(hit, rf, rank_ref[:, sl])
            new_vals.append(jnp.where(slot == rf, m, vals[:, sl]))
        return jnp.concatenate(new_vals, axis=1)

    return lax.fori_loop(0, count, body, jnp.zeros((count, width), jnp.float32))


def _topk_kernel(s_ref, r2_ref, e2_ref, ell_ref, w_ref, cur_ref, rank_ref, cand_ref, crank_ref):
    k = PEER_TOPK
    nchunk = r2_ref.shape[1]
    tl = nchunk * LANES
    cur_ref[:, :tl] = s_ref[0]
    cur_ref[:, tl:] = s_ref[1]
    vals = _extract_top(cur_ref, rank_ref, k)
    a = vals[:, :tl]
    b = vals[:, tl:]
    sub = lax.broadcasted_iota(jnp.int32, (SUBLANES, tl), 0)
    cand_ref[0:SUBLANES] = a[0:1] + b[0:SUBLANES]
    cand_ref[SUBLANES:2 * SUBLANES] = a[0:1] + b[SUBLANES:2 * SUBLANES]
    for i in range(1, SUBLANES):
        cand_ref[(i + 1) * SUBLANES:(i + 2) * SUBLANES] = jnp.where(
            sub < k // (i + 1), a[i:i + 1] + b[0:SUBLANES], NEG_INF)
    base = (SUBLANES + 1) * SUBLANES
    cand_ref[base:base + SUBLANES] = a[SUBLANES:] + b[0:1]
    cvals = _extract_top(cand_ref, crank_ref, k)
    taken = jnp.where(crank_ref[...] < float(k), 1.0, 0.0)
    z = jnp.sum(jnp.exp(cvals - cvals[0:1]), axis=0, keepdims=True)
    rank1 = rank_ref[:, :tl]
    ell = jnp.where(rank1 == 0.0, jnp.sum(taken[0:2 * SUBLANES], axis=0, keepdims=True), 0.0)
    for i in range(1, SUBLANES):
        lim = jnp.sum(taken[(i + 1) * SUBLANES:(i + 2) * SUBLANES], axis=0, keepdims=True)
        ell = jnp.where(rank1 == float(i), lim, ell)
    for j in range(SUBLANES):
        ell = jnp.where(rank1 == float(SUBLANES + j), taken[base + j:base + j + 1], ell)
    e2 = jnp.exp(s_ref[1] - b[0:1])
    w = jnp.exp(s_ref[0] - a[0:1]) * (0.5 / z)
    for c in range(nchunk):
        sl = slice(c * LANES, (c + 1) * LANES)
        r2_ref[0, c] = rank_ref[:, tl + c * LANES:tl + (c + 1) * LANES].astype(r2_ref.dtype)
        e2_ref[0, c] = e2[:, sl].astype(e2_ref.dtype)
        ell_ref[0, c] = ell[:, sl]
        w_ref[0, c] = w[:, sl]


def peer_topk(scores, n_heads, *, tl=512):
    hs, nk, t = scores.shape
    assert hs == 2 * n_heads and PEER_TOPK == 2 * SUBLANES
    tl = min(tl, t)
    assert t % tl == 0 and tl % LANES == 0
    nchunk = tl // LANES
    ncand = (SUBLANES + 2) * SUBLANES
    shape = (n_heads, t // LANES, nk, LANES)
    ospec = pl.BlockSpec((1, nchunk, nk, LANES), lambda h, i: (h, i, 0, 0))
    return pl.pallas_call(
        _topk_kernel,
        grid=(n_heads, t // tl),
        in_specs=[pl.BlockSpec((2, nk, tl), lambda h, i: (h, 0, i))],
        out_specs=[ospec] * 4,
        out_shape=[jax.ShapeDtypeStruct(shape, jnp.float32)] * 4,
        scratch_shapes=[pltpu.VMEM((nk, 2 * tl), jnp.float32), pltpu.VMEM((nk, 2 * tl), jnp.float32),
                        pltpu.VMEM((ncand, tl), jnp.float32), pltpu.VMEM((ncand, tl), jnp.float32)],
        compiler_params=_cparams("parallel", "parallel"),
        name="peer_topk",
    )(scores)


def _peer_hidden_kernel(u_ref, xn_ref, o_ref, ub_ref):
    @pl.when(pl.program_id(1) == 0)
    def _():
        ub_ref[...] = u_ref[...].astype(jnp.bfloat16)

    o_ref[...] = lax.dot_general(ub_ref[...], xn_ref[...], NT_DIMS, preferred_element_type=jnp.float32)


def peer_hidden(xn, u, *, tb=1024, eb=1024):
    t, d = xn.shape
    e = u.shape[0]
    tb = min(tb, t)
    assert t % tb == 0 and e % eb == 0
    return pl.pallas_call(
        _peer_hidden_kernel,
        grid=(e // eb, t // tb),
        in_specs=[
            pl.BlockSpec((eb, d), lambda j, i: (j, 0)),
            pl.BlockSpec((tb, d), lambda j, i: (i, 0)),
        ],
        out_specs=pl.BlockSpec((eb, tb), lambda j, i: (j, i)),
        out_shape=jax.ShapeDtypeStruct((e, t), jnp.float32),
        scratch_shapes=[pltpu.VMEM((eb, d), jnp.bfloat16)],
        compiler_params=_cparams("parallel", "arbitrary"),
        name="peer_hidden",
    )(u, xn)


def _peer_dense_kernel(h_ref, vt_ref, r2_ref, e2_ref, ell_ref, w_ref, o_ref, a_ref):
    n_heads, nchunk, nk, _ = r2_ref.shape
    na = ell_ref.shape[2]

    @pl.when(pl.program_id(1) == 0)
    def _():
        o_ref[...] = jnp.zeros_like(o_ref)

    for ai in range(na):
        for c in range(nchunk):
            g = jnp.zeros((nk, LANES), jnp.float32)
            for h in range(n_heads):
                lim = ell_ref[h, c, ai:ai + 1, :]
                wa = w_ref[h, c, ai:ai + 1, :]
                g = g + jnp.where(r2_ref[h, c] < lim, e2_ref[h, c], 0.0) * wa
            hid = h_ref[ai * nk:(ai + 1) * nk, c * LANES:(c + 1) * LANES]
            act = hid * (1.0 + lax.erf(hid * INV_SQRT2))
            a_ref[ai * nk:(ai + 1) * nk, c * LANES:(c + 1) * LANES] = (act * g).astype(jnp.bfloat16)
    o_ref[...] += jnp.dot(vt_ref[...], a_ref[...], preferred_element_type=jnp.float32)


def peer_dense(hid_t, vt, r2, e2, ell, w, *, tb=512, na=8):
    e, t = hid_t.shape
    d = vt.shape[0]
    n_heads, _, nk, _ = r2.shape
    eb = na * nk
    tb = min(tb, t)
    assert t % tb == 0 and tb % LANES == 0 and e % eb == 0 and e == nk * nk and na % SUBLANES == 0
    nchunk = tb // LANES
    key_spec = pl.BlockSpec((n_heads, nchunk, nk, LANES), lambda i, j: (0, i, 0, 0))
    row_spec = pl.BlockSpec((n_heads, nchunk, na, LANES), lambda i, j: (0, i, j, 0))
    return pl.pallas_call(
        _peer_dense_kernel,
        grid=(t // tb, e // eb),
        in_specs=[
            pl.BlockSpec((eb, tb), lambda i, j: (j, i)),
            pl.BlockSpec((d, eb), lambda i, j: (0, j)),
            key_spec, key_spec, row_spec, row_spec,
        ],
        out_specs=pl.BlockSpec((d, tb), lambda i, j: (0, i)),
        out_shape=jax.ShapeDtypeStruct((d, t), jnp.float32),
        scratch_shapes=[pltpu.VMEM((eb, tb), jnp.bfloat16)],
        compiler_params=_cparams("parallel", "arbitrary"),
        name="peer_dense",
    )(hid_t, vt, r2, e2, ell, w)


def _final_kernel(x_ref, pt_ref, g_ref, o_ref):
    x = x_ref[...] + pt_ref[...].T
    inv = lax.rsqrt(jnp.mean(x * x, axis=-1, keepdims=True) + RMS_EPS)
    o_ref[...] = x * inv * g_ref[...]


def final_norm(x1, peer_t, gain, *, tm=256):
    t, d = x1.shape
    tm = min(tm, t)
    assert t % tm == 0
    return pl.pallas_call(
        _final_kernel,
        grid=(t // tm,),
        in_specs=[
            pl.BlockSpec((tm, d), lambda i: (i, 0)),
            pl.BlockSpec((d, tm), lambda i: (0, i)),
            pl.BlockSpec((1, d), lambda i: (0, 0)),
        ],
        out_specs=pl.BlockSpec((tm, d), lambda i: (i, 0)),
        out_shape=jax.ShapeDtypeStruct((t, d), jnp.float32),
        compiler_params=_cparams("parallel"),
        name="final_norm",
    )(x1, peer_t, gain.reshape(1, d))


ATT_BLK = 512


def _layer(x, norm_mix, w_in, conv_w, w_conv_out, w_attn_out, w_out, norm_ffn,
           peer_wq, peer_subkeys, peer_u, peer_v):
    t, d = x.shape
    c = conv_w.shape[1]
    a = w_attn_out.shape[0]
    n_keys, half = peer_subkeys.shape[1:]
    peer_heads = peer_wq.shape[1] // (2 * half)
    dh = LANES // 2
    attn_heads = a // dh
    bf = jnp.bfloat16

    xn = rmsnorm(x, norm_mix, bf)
    pc = matmul(xn, w_in, 0, 3 * c, jnp.float32)
    q = matmul(xn, w_in, 3 * c, a, bf, scale=dh ** -0.5)
    k = matmul(xn, w_in, 3 * c + a, a, bf)
    vt = matmul_nt(w_in[:, 3 * c + 2 * a:3 * c + 3 * a].T.astype(bf), xn, bf, tm=min(ATT_BLK, t))
    gates = matmul(xn, w_in, 3 * c + 3 * a, 2 * d, jnp.float32)

    yc = conv_mixer(pc, conv_w)
    ya = attention(q, k, vt, attn_heads, dh, blk=min(ATT_BLK, t))
    merged = merge(yc, ya, w_conv_out.astype(bf), w_attn_out.astype(bf), gates)
    x1 = out_proj(merged, w_out.astype(bf), x)

    xn2 = rmsnorm(x1, norm_ffn, bf)
    qp = matmul(xn2, peer_wq, 0, peer_wq.shape[1], jnp.float32)
    scores = peer_scores(qp, peer_subkeys.astype(bf), peer_heads)
    r2, e2, ell, w = peer_topk(scores, peer_heads)
    hid_t = peer_hidden(xn2, peer_u)
    peer_t = peer_dense(hid_t, peer_v.T.astype(bf), r2, e2, ell, w)
    return x1, peer_t


def kernel(x, norm_mix, w_in, conv_w, w_conv_out, w_attn_out, w_out, norm_ffn, peer_wq,
           peer_subkeys, peer_u, peer_v, norm_final):
    b, t, d = x.shape
    depth = norm_mix.shape[0]
    outs = []
    for bi in range(b):
        xb = x[bi]
        peer_t = None
        for layer in range(depth):
            if peer_t is not None:
                xb = xb + peer_t.T
            xb, peer_t = _layer(xb, norm_mix[layer], w_in[layer], conv_w[layer], w_conv_out[layer],
                                w_attn_out[layer], w_out[layer], norm_ffn[layer], peer_wq[layer],
                                peer_subkeys[layer], peer_u[layer], peer_v[layer])
        outs.append(final_norm(xb, peer_t, norm_final))
    return jnp.stack(outs, axis=0)
```

```python
import functools

import jax
import jax.numpy as jnp
from jax import lax
from jax.experimental import pallas as pl
from jax.experimental.pallas import tpu as pltpu

RMS_EPS = 1e-6
LANES = 128
SUBLANES = 8
BF16_ROWS = 16
VMEM_LIMIT = 56 * 1024 * 1024
PEER_TOPK = 16
NEG_INF = float("-inf")
INV_SQRT2 = 0.7071067811865476
LOG2E = 1.4426950408889634
NT_DIMS = (((1,), (1,)), ((), ()))


def _cparams(*sem):
    return pltpu.CompilerParams(dimension_semantics=sem, vmem_limit_bytes=VMEM_LIMIT)


def _rmsnorm_kernel(x_ref, g_ref, o_ref):
    x = x_ref[...]
    inv = lax.rsqrt(jnp.mean(x * x, axis=-1, keepdims=True) + RMS_EPS)
    o_ref[...] = (x * inv * g_ref[...]).astype(o_ref.dtype)


def rmsnorm(x, gain, out_dtype, *, tm=512):
    t, d = x.shape
    tm = min(tm, t)
    assert t % tm == 0
    return pl.pallas_call(
        _rmsnorm_kernel,
        grid=(t // tm,),
        in_specs=[pl.BlockSpec((tm, d), lambda i: (i, 0)), pl.BlockSpec((1, d), lambda i: (0, 0))],
        out_specs=pl.BlockSpec((tm, d), lambda i: (i, 0)),
        out_shape=jax.ShapeDtypeStruct((t, d), out_dtype),
        compiler_params=_cparams("parallel"),
        name="rmsnorm",
    )(x, gain.reshape(1, d))


def _matmul_kernel(x_ref, w_ref, o_ref, wb_ref, *, scale):
    @pl.when(pl.program_id(1) == 0)
    def _():
        wb_ref[...] = w_ref[...].astype(jnp.bfloat16)

    acc = jnp.dot(x_ref[...], wb_ref[...], preferred_element_type=jnp.float32)
    if scale != 1.0:
        acc = acc * scale
    o_ref[...] = acc.astype(o_ref.dtype)


def matmul(x, w, col0, ncols, out_dtype, *, scale=1.0, tm=1024, tn=1024):
    t, k = x.shape
    tm = min(tm, t)
    tn = min(tn, ncols)
    assert t % tm == 0 and ncols % tn == 0 and col0 % tn == 0
    jb0 = col0 // tn
    return pl.pallas_call(
        functools.partial(_matmul_kernel, scale=scale),
        grid=(ncols // tn, t // tm),
        in_specs=[
            pl.BlockSpec((tm, k), lambda j, i: (i, 0)),
            pl.BlockSpec((k, tn), lambda j, i: (0, jb0 + j)),
        ],
        out_specs=pl.BlockSpec((tm, tn), lambda j, i: (i, j)),
        out_shape=jax.ShapeDtypeStruct((t, ncols), out_dtype),
        scratch_shapes=[pltpu.VMEM((k, tn), jnp.bfloat16)],
        compiler_params=_cparams("parallel", "arbitrary"),
        name="matmul",
    )(x, w)


def _matmul_nt_kernel(wt_ref, x_ref, o_ref):
    o_ref[0] = lax.dot_general(wt_ref[...], x_ref[...], NT_DIMS,
                               preferred_element_type=jnp.float32).astype(o_ref.dtype)


def matmul_nt(wt, x, out_dtype, *, tm, tn=1024):
    t, k = x.shape
    n = wt.shape[0]
    tn = min(tn, n)
    assert t % tm == 0 and n % tn == 0
    return pl.pallas_call(
        _matmul_nt_kernel,
        grid=(t // tm, n // tn),
        in_specs=[
            pl.BlockSpec((tn, k), lambda i, j: (j, 0)),
            pl.BlockSpec((tm, k), lambda i, j: (i, 0)),
        ],
        out_specs=pl.BlockSpec((1, tn, tm), lambda i, j: (i, j, 0)),
        out_shape=jax.ShapeDtypeStruct((t // tm, n, tm), out_dtype),
        compiler_params=_cparams("parallel", "parallel"),
        name="matmul_nt",
    )(wt, x)


def _conv_kernel(cb_ref, cc_ref, ch_ref, cch_ref, chh_ref, w_ref, o_ref, z_ref):
    tm = cb_ref.shape[0]
    z_halo = cch_ref[...] * chh_ref[...]
    z_ref[0:SUBLANES, :] = jnp.where(pl.program_id(0) == 0, 0.0, z_halo)
    z = cc_ref[...] * ch_ref[...]
    z_ref[SUBLANES:, :] = z
    w = w_ref[...]
    k = w.shape[0]
    y = z * w[k - 1:k, :]
    for j in range(1, k):
        y = y + z_ref[pl.ds(SUBLANES - j, tm), :] * w[k - 1 - j:k - j, :]
    o_ref[...] = (cb_ref[...] * y).astype(o_ref.dtype)


def conv_mixer(pc, conv_w, *, tm=512):
    t, c3 = pc.shape
    c = c3 // 3
    k = conv_w.shape[0]
    assert k - 1 <= SUBLANES
    tm = min(tm, t)
    assert t % tm == 0 and tm % SUBLANES == 0
    hb = tm // SUBLANES

    def halo_map(col):
        return lambda i: (jnp.maximum(i * hb - 1, 0), col)

    return pl.pallas_call(
        _conv_kernel,
        grid=(t // tm,),
        in_specs=[
            pl.BlockSpec((tm, c), lambda i: (i, 0)),
            pl.BlockSpec((tm, c), lambda i: (i, 1)),
            pl.BlockSpec((tm, c), lambda i: (i, 2)),
            pl.BlockSpec((SUBLANES, c), halo_map(1)),
            pl.BlockSpec((SUBLANES, c), halo_map(2)),
            pl.BlockSpec((k, c), lambda i: (0, 0)),
        ],
        out_specs=pl.BlockSpec((tm, c), lambda i: (i, 0)),
        out_shape=jax.ShapeDtypeStruct((t, c), jnp.bfloat16),
        scratch_shapes=[pltpu.VMEM((tm + SUBLANES, c), jnp.float32)],
        compiler_params=_cparams("parallel"),
        name="conv_mixer",
    )(pc, pc, pc, pc, pc, conv_w)


ATT_SUB = 128


def _attn_kernel(q_ref, k_ref, vt_ref, tri_ref, o_ref, acc_ref, *, dh):
    blk = q_ref.shape[0]
    nheads = 2 * (q_ref.shape[1] // LANES)
    nsub = blk // ATT_SUB
    i = pl.program_id(1)
    lane = lax.broadcasted_iota(jnp.int32, (1, LANES), 1)
    head0 = lane < dh
    qh = []
    for g in range(nheads // 2):
        q2 = q_ref[:, g * LANES:(g + 1) * LANES]
        zero = jnp.zeros_like(q2)
        qh += [jnp.where(head0, q2, zero), jnp.where(head0, zero, q2)]
    tri = tri_ref[...]
    acc_ref[...] = jnp.zeros_like(acc_ref)

    def block(jb, cars, diagonal):
        kb = k_ref[pl.ds(pl.multiple_of(jb * blk, blk), blk), :]
        vt = vt_ref[jb]
        new_cars = []
        zts = [lax.dot_general(kb[:, (h // 2) * LANES:(h // 2 + 1) * LANES], qh[h], NT_DIMS,
                               preferred_element_type=jnp.float32)
               for h in range(nheads)]
        for h in range(nheads):
            car = cars[h]
            ws = [None] * nsub
            for s in range(nsub - 1, -1, -1):
                z = zts[h][s * ATT_SUB:(s + 1) * ATT_SUB]
                sp = jnp.maximum(z, 0.0) + jnp.log(1.0 + jnp.exp2(jnp.abs(z) * -LOG2E))
                logsig = z - sp
                if diagonal:
                    key = lax.broadcasted_iota(jnp.int32, z.shape, 0) + s * ATT_SUB
                    qry = lax.broadcasted_iota(jnp.int32, z.shape, 1)
                    mask = key < qry
                    sp = jnp.where(mask, sp, 0.0)
                r = jnp.dot(tri, sp.astype(jnp.bfloat16), preferred_element_type=jnp.float32)
                sfx = r[:ATT_SUB]
                tot = r[ATT_SUB:ATT_SUB + 1]
                w = jnp.exp(logsig - sfx - car)
                if diagonal:
                    w = jnp.where(mask, w, 0.0)
                ws[s] = w.astype(jnp.bfloat16)
                car = car + tot
            wt = jnp.concatenate(ws, axis=0)
            acc_ref[h * dh:(h + 1) * dh, :] += jnp.dot(vt[h * dh:(h + 1) * dh, :], wt,
                                                       preferred_element_type=jnp.float32)
            new_cars.append(car)
        return tuple(new_cars)

    car0 = jnp.zeros((1, blk), jnp.float32)
    cars = block(i, (car0,) * nheads, True)

    def body(n, cars):
        return block(i - 1 - n, cars, False)

    lax.fori_loop(0, i, body, cars)
    o_ref[...] = acc_ref[...].T.astype(o_ref.dtype)


def _tri_matrix():
    s = jnp.arange(ATT_SUB)[:, None]
    j = jnp.arange(ATT_SUB)[None, :]
    later = (j > s).astype(jnp.bfloat16)
    ones = jnp.ones((BF16_ROWS, ATT_SUB), jnp.bfloat16)
    return jnp.concatenate([later, ones], axis=0)


def attention(q, k, vt, n_heads, dh, *, blk=512, heads_per_step=4):
    t = q.shape[0]
    assert 2 * dh == LANES and heads_per_step % 2 == 0 and dh % BF16_ROWS == 0
    assert n_heads % heads_per_step == 0
    assert t % blk == 0 and blk % ATT_SUB == 0 and vt.shape == (t // blk, n_heads * dh, blk)
    ngroup = n_heads // heads_per_step
    gw = heads_per_step * dh
    nb = t // blk
    return pl.pallas_call(
        functools.partial(_attn_kernel, dh=dh),
        grid=(ngroup, nb),
        in_specs=[
            pl.BlockSpec((blk, gw), lambda p, i: (i, p)),
            pl.BlockSpec((t, gw), lambda p, i: (0, p)),
            pl.BlockSpec((nb, gw, blk), lambda p, i: (0, p, 0)),
            pl.BlockSpec((ATT_SUB + BF16_ROWS, ATT_SUB), lambda p, i: (0, 0)),
        ],
        out_specs=pl.BlockSpec((blk, gw), lambda p, i: (i, p)),
        out_shape=jax.ShapeDtypeStruct((t, n_heads * dh), jnp.bfloat16),
        scratch_shapes=[pltpu.VMEM((gw, blk), jnp.float32)],
        compiler_params=_cparams("parallel", "parallel"),
        name="stickbreak_attention",
    )(q, k, vt, _tri_matrix())


def _merge_kernel(yc_ref, ya_ref, wc_ref, wa_ref, gc_ref, ga_ref, o_ref):
    yc = jnp.dot(yc_ref[...], wc_ref[...], preferred_element_type=jnp.float32)
    ya = jnp.dot(ya_ref[...], wa_ref[...], preferred_element_type=jnp.float32)
    o_ref[...] = (jax.nn.sigmoid(gc_ref[...]) * yc + jax.nn.sigmoid(ga_ref[...]) * ya).astype(o_ref.dtype)


def merge(yc, ya, wc, wa, gates, *, tm=1024, tn=1024):
    t, c = yc.shape
    a = ya.shape[1]
    d = wc.shape[1]
    tm = min(tm, t)
    tn = min(tn, d)
    assert t % tm == 0 and d % tn == 0
    nj = d // tn
    return pl.pallas_call(
        _merge_kernel,
        grid=(t // tm, nj),
        in_specs=[
            pl.BlockSpec((tm, c), lambda i, j: (i, 0)),
            pl.BlockSpec((tm, a), lambda i, j: (i, 0)),
            pl.BlockSpec((c, tn), lambda i, j: (0, j)),
            pl.BlockSpec((a, tn), lambda i, j: (0, j)),
            pl.BlockSpec((tm, tn), lambda i, j: (i, j)),
            pl.BlockSpec((tm, tn), lambda i, j: (i, nj + j)),
        ],
        out_specs=pl.BlockSpec((tm, tn), lambda i, j: (i, j)),
        out_shape=jax.ShapeDtypeStruct((t, d), jnp.bfloat16),
        compiler_params=_cparams("parallel", "parallel"),
        name="merge_mixers",
    )(yc, ya, wc, wa, gates, gates)


def _out_proj_kernel(m_ref, w_ref, x_ref, o_ref):
    o_ref[...] = x_ref[...] + jnp.dot(m_ref[...], w_ref[...], preferred_element_type=jnp.float32)


def out_proj(merged, w, x, *, tm=1024, tn=1024):
    t, d = merged.shape
    n = w.shape[1]
    tm = min(tm, t)
    tn = min(tn, n)
    assert t % tm == 0 and n % tn == 0
    return pl.pallas_call(
        _out_proj_kernel,
        grid=(t // tm, n // tn),
        in_specs=[
            pl.BlockSpec((tm, d), lambda i, j: (i, 0)),
            pl.BlockSpec((d, tn), lambda i, j: (0, j)),
            pl.BlockSpec((tm, tn), lambda i, j: (i, j)),
        ],
        out_specs=pl.BlockSpec((tm, tn), lambda i, j: (i, j)),
        out_shape=jax.ShapeDtypeStruct((t, n), jnp.float32),
        compiler_params=_cparams("parallel", "parallel"),
        name="out_proj",
    )(merged, w, x)


def _scores_kernel(x_ref, w_ref, k_ref, o_ref, wb_ref):
    @pl.when(pl.program_id(1) == 0)
    def _():
        wb_ref[...] = w_ref[...].astype(jnp.bfloat16)

    half = k_ref.shape[2]
    q = jnp.dot(x_ref[...], wb_ref[...], preferred_element_type=jnp.float32)
    for g in range(o_ref.shape[0]):
        qg = q[:, g * half:(g + 1) * half].astype(jnp.bfloat16)
        keys = k_ref[g % 2]
        o_ref[g] = lax.dot_general(keys, qg, NT_DIMS, preferred_element_type=jnp.float32)


def peer_scores(xn, wq, subkeys, n_heads, *, tm=1024, tn=1024):
    t, d = xn.shape
    _, nk, half = subkeys.shape
    qd = wq.shape[1]
    assert qd == n_heads * 2 * half
    tm = min(tm, t)
    tn = min(tn, qd)
    assert t % tm == 0 and qd % tn == 0 and tn % (2 * half) == 0
    gpt = tn // half
    return pl.pallas_call(
        _scores_kernel,
        grid=(qd // tn, t // tm),
        in_specs=[
            pl.BlockSpec((tm, d), lambda j, i: (i, 0)),
            pl.BlockSpec((d, tn), lambda j, i: (0, j)),
            pl.BlockSpec((2, nk, half), lambda j, i: (0, 0, 0)),
        ],
        out_specs=pl.BlockSpec((gpt, nk, tm), lambda j, i: (j, 0, i)),
        out_shape=jax.ShapeDtypeStruct((2 * n_heads, nk, t), jnp.float32),
        scratch_shapes=[pltpu.VMEM((d, tn), jnp.bfloat16)],
        compiler_params=_cparams("parallel", "arbitrary"),
        name="peer_scores",
    )(xn, wq, subkeys)


def _extract_top(cur_ref, rank_ref, count):
    rows, width = cur_ref.shape
    ntile = width // LANES
    row = lax.broadcasted_iota(jnp.int32, (rows, LANES), 0).astype(jnp.float32)
    slot = lax.broadcasted_iota(jnp.int32, (count, LANES), 0).astype(jnp.float32)

    def body(r, carry):
        vals, firsts = carry
        rf = lax.convert_element_type(r, jnp.float32)
        new_vals, new_firsts = [], []
        for tile in range(ntile):
            sl = slice(tile * LANES, (tile + 1) * LANES)
            cur = cur_ref[:, sl]
            m = jnp.max(cur, axis=0, keepdims=True)
            first = jnp.min(jnp.where(cur == m, row, float(rows)), axis=0, keepdims=True)
            cur_ref[:, sl] = jnp.where(row == first, NEG_INF, cur)
            new_vals.append(jnp.where(slot == rf, m, vals[:, sl]))
            new_firsts.append(jnp.where(slot == rf, first, firsts[:, sl]))
        return jnp.concatenate(new_vals, axis=1), jnp.concatenate(new_firsts, axis=1)

    init = jnp.zeros((count, width), jnp.float32)
    vals, firsts = lax.fori_loop(0, count, body, (init, init))
    for tile in range(ntile):
        sl = slice(tile * LANES, (tile + 1) * LANES)
        rank = jnp.full((rows, LANES), float(count), jnp.float32)
        for r in range(count):
            rank = jnp.where(row == firsts[r:r + 1, sl], float(r), rank)
        rank_ref[:, sl] = rank
    return vals


def _topk_kernel(s_ref, r2_ref, e2_ref, ell_ref, w_ref, cur_ref, rank_ref, cand_ref, crank_ref):
    k = PEER_TOPK
    nchunk = r2_ref.shape[1]
    tl = nchunk * LANES
    cur_ref[:, :tl] = s_ref[0]
    cur_ref[:, tl:] = s_ref[1]
    vals = _extract_top(cur_ref, rank_ref, k)
    a = vals[:, :tl]
    b = vals[:, tl:]
    sub = lax.broadcasted_iota(jnp.int32, (SUBLANES, tl), 0)
    cand_ref[0:SUBLANES] = a[0:1] + b[0:SUBLANES]
    cand_ref[SUBLANES:2 * SUBLANES] = a[0:1] + b[SUBLANES:2 * SUBLANES]
    for i in range(1, SUBLANES):
        cand_ref[(i + 1) * SUBLANES:(i + 2) * SUBLANES] = jnp.where(
            sub < k // (i + 1), a[i:i + 1] + b[0:SUBLANES], NEG_INF)
    base = (SUBLANES + 1) * SUBLANES
    cand_ref[base:base + SUBLANES] = a[SUBLANES:] + b[0:1]
    cvals = _extract_top(cand_ref, crank_ref, k)
    taken = jnp.where(crank_ref[...] < float(k), 1.0, 0.0)
    z = jnp.sum(jnp.exp(cvals - cvals[0:1]), axis=0, keepdims=True)
    rank1 = rank_ref[:, :tl]
    ell = jnp.where(rank1 == 0.0, jnp.sum(taken[0:2 * SUBLANES], axis=0, keepdims=True), 0.0)
    for i in range(1, SUBLANES):
        lim = jnp.sum(taken[(i + 1) * SUBLANES:(i + 2) * SUBLANES], axis=0, keepdims=True)
        ell = jnp.where(rank1 == float(i), lim, ell)
    for j in range(SUBLANES):
        ell = jnp.where(rank1 == float(SUBLANES + j), taken[base + j:base + j + 1], ell)
    e2 = jnp.exp(s_ref[1] - b[0:1])
    w = jnp.exp(s_ref[0] - a[0:1]) * (0.5 / z)
    for c in range(nchunk):
        sl = slice(c * LANES, (c + 1) * LANES)
        r2_ref[0, c] = rank_ref[:, tl + c * LANES:tl + (c + 1) * LANES].astype(r2_ref.dtype)
        e2_ref[0, c] = e2[:, sl].astype(e2_ref.dtype)
        ell_ref[0, c] = ell[:, sl]
        w_ref[0, c] = w[:, sl]


def peer_topk(scores, n_heads, *, tl=512):
    hs, nk, t = scores.shape
    assert hs == 2 * n_heads and PEER_TOPK == 2 * SUBLANES
    tl = min(tl, t)
    assert t % tl == 0 and tl % LANES == 0
    nchunk = tl // LANES
    ncand = (SUBLANES + 2) * SUBLANES
    shape = (n_heads, t // LANES, nk, LANES)
    ospec = pl.BlockSpec((1, nchunk, nk, LANES), lambda h, i: (h, i, 0, 0))
    return pl.pallas_call(
        _topk_kernel,
        grid=(n_heads, t // tl),
        in_specs=[pl.BlockSpec((2, nk, tl), lambda h, i: (h, 0, i))],
        out_specs=[ospec] * 4,
        out_shape=[jax.ShapeDtypeStruct(shape, jnp.float32)] * 4,
        scratch_shapes=[pltpu.VMEM((nk, 2 * tl), jnp.float32), pltpu.VMEM((nk, 2 * tl), jnp.float32),
                        pltpu.VMEM((ncand, tl), jnp.float32), pltpu.VMEM((ncand, tl), jnp.float32)],
        compiler_params=_cparams("parallel", "parallel"),
        name="peer_topk",
    )(scores)


def _peer_hidden_kernel(u_ref, xn_ref, o_ref, ub_ref):
    @pl.when(pl.program_id(1) == 0)
    def _():
        ub_ref[...] = u_ref[...].astype(jnp.bfloat16)

    hid = lax.dot_general(ub_ref[...], xn_ref[...], NT_DIMS, preferred_element_type=jnp.float32)
    o_ref[...] = hid * (1.0 + lax.erf(hid * INV_SQRT2))


def peer_hidden(xn, u, *, tb=1024, eb=1024):
    t, d = xn.shape
    e = u.shape[0]
    tb = min(tb, t)
    assert t % tb == 0 and e % eb == 0
    return pl.pallas_call(
        _peer_hidden_kernel,
        grid=(e // eb, t // tb),
        in_specs=[
            pl.BlockSpec((eb, d), lambda j, i: (j, 0)),
            pl.BlockSpec((tb, d), lambda j, i: (i, 0)),
        ],
        out_specs=pl.BlockSpec((eb, tb), lambda j, i: (j, i)),
        out_shape=jax.ShapeDtypeStruct((e, t), jnp.float32),
        scratch_shapes=[pltpu.VMEM((eb, d), jnp.bfloat16)],
        compiler_params=_cparams("parallel", "arbitrary"),
        name="peer_hidden",
    )(u, xn)


GATE_GROUP = 4


def _peer_dense_kernel(h_ref, vt_ref, r2_ref, e2_ref, ell_ref, w_ref, x_ref, gain_ref, o_ref,
                       acc_ref, a_ref, *, final_norm):
    n_heads, nchunk, nk, _ = r2_ref.shape
    na = ell_ref.shape[2]
    j = pl.program_id(1)

    @pl.when(j == 0)
    def _():
        acc_ref[...] = jnp.zeros_like(acc_ref)

    hb = nk // 2
    for c in range(nchunk):
        lanes = slice(c * LANES, (c + 1) * LANES)
        for half in range(2):
            keys = slice(half * hb, (half + 1) * hb)
            for a0 in range(0, na, GATE_GROUP):
                group = range(a0, a0 + GATE_GROUP)
                g = {ai: jnp.zeros((hb, LANES), jnp.float32) for ai in group}
                for h in range(n_heads):
                    r2 = r2_ref[h, c, keys, :]
                    e2 = e2_ref[h, c, keys, :]
                    for ai in group:
                        lim = ell_ref[h, c, ai:ai + 1, :]
                        wa = w_ref[h, c, ai:ai + 1, :]
                        g[ai] = g[ai] + jnp.where(r2 < lim, e2, 0.0) * wa
                for ai in group:
                    rows = slice(ai * nk + half * hb, ai * nk + (half + 1) * hb)
                    act = h_ref[rows, lanes]
                    a_ref[rows, lanes] = (act * g[ai]).astype(jnp.bfloat16)
    acc_ref[...] += jnp.dot(vt_ref[...], a_ref[...], preferred_element_type=jnp.float32)

    @pl.when(j == pl.num_programs(1) - 1)
    def _():
        x = x_ref[...] + acc_ref[...].T
        if final_norm:
            inv = lax.rsqrt(jnp.mean(x * x, axis=-1, keepdims=True) + RMS_EPS)
            x = x * inv * gain_ref[...]
        o_ref[...] = x


def peer_dense(act_t, vt, r2, e2, ell, w, x, gain, *, final_norm, tb=512, na=8):
    e, t = act_t.shape
    d = vt.shape[0]
    n_heads, _, nk, _ = r2.shape
    eb = na * nk
    tb = min(tb, t)
    assert t % tb == 0 and tb % LANES == 0 and e % eb == 0 and e == nk * nk and na % SUBLANES == 0
    nchunk = tb // LANES
    key_spec = pl.BlockSpec((n_heads, nchunk, nk, LANES), lambda i, j: (0, i, 0, 0))
    row_spec = pl.BlockSpec((n_heads, nchunk, na, LANES), lambda i, j: (0, i, j, 0))
    return pl.pallas_call(
        functools.partial(_peer_dense_kernel, final_norm=final_norm),
        grid=(t // tb, e // eb),
        in_specs=[
            pl.BlockSpec((eb, tb), lambda i, j: (j, i)),
            pl.BlockSpec((d, eb), lambda i, j: (0, j)),
            key_spec, key_spec, row_spec, row_spec,
            pl.BlockSpec((tb, d), lambda i, j: (i, 0)),
            pl.BlockSpec((1, d), lambda i, j: (0, 0)),
        ],
        out_specs=pl.BlockSpec((tb, d), lambda i, j: (i, 0)),
        out_shape=jax.ShapeDtypeStruct((t, d), jnp.float32),
        scratch_shapes=[pltpu.VMEM((d, tb), jnp.float32), pltpu.VMEM((eb, tb), jnp.bfloat16)],
        compiler_params=_cparams("parallel", "arbitrary"),
        name="peer_dense",
    )(act_t, vt, r2, e2, ell, w, x, gain.reshape(1, d))


ATT_BLK = 512


def _layer(x, norm_mix, w_in, conv_w, w_conv_out, w_attn_out, w_out, norm_ffn,
           peer_wq, peer_subkeys, peer_u, peer_v, norm_final, is_last):
    t, d = x.shape
    c = conv_w.shape[1]
    a = w_attn_out.shape[0]
    n_keys, half = peer_subkeys.shape[1:]
    peer_heads = peer_wq.shape[1] // (2 * half)
    dh = LANES // 2
    attn_heads = a // dh
    bf = jnp.bfloat16

    xn = rmsnorm(x, norm_mix, bf)
    pc = matmul(xn, w_in, 0, 3 * c, jnp.float32)
    q = matmul(xn, w_in, 3 * c, a, bf, scale=dh ** -0.5)
    k = matmul(xn, w_in, 3 * c + a, a, bf)
    vt = matmul_nt(w_in[:, 3 * c + 2 * a:3 * c + 3 * a].T.astype(bf), xn, bf, tm=min(ATT_BLK, t))
    gates = matmul(xn, w_in, 3 * c + 3 * a, 2 * d, jnp.float32)

    yc = conv_mixer(pc, conv_w)
    ya = attention(q, k, vt, attn_heads, dh, blk=min(ATT_BLK, t))
    merged = merge(yc, ya, w_conv_out.astype(bf), w_attn_out.astype(bf), gates)
    x1 = out_proj(merged, w_out.astype(bf), x)

    xn2 = rmsnorm(x1, norm_ffn, bf)
    scores = peer_scores(xn2, peer_wq, peer_subkeys.astype(bf), peer_heads)
    r2, e2, ell, w = peer_topk(scores, peer_heads)
    act_t = peer_hidden(xn2, peer_u)
    return peer_dense(act_t, peer_v.T.astype(bf), r2, e2, ell, w, x1, norm_final, final_norm=is_last)


def kernel(x, norm_mix, w_in, conv_w, w_conv_out, w_attn_out, w_out, norm_ffn, peer_wq,
           peer_subkeys, peer_u, peer_v, norm_final):
    b, t, d = x.shape
    depth = norm_mix.shape[0]
    outs = []
    for bi in range(b):
        xb = x[bi]
        for layer in range(depth):
            xb = _layer(xb, norm_mix[layer], w_in[layer], conv_w[layer], w_conv_out[layer],
                        w_attn_out[layer], w_out[layer], norm_ffn[layer], peer_wq[layer],
                        peer_subkeys[layer], peer_u[layer], peer_v[layer], norm_final,
                        is_last=layer == depth - 1)
        outs.append(xb)
    return jnp.stack(outs, axis=0)
```

```python
import functools

import jax
import jax.numpy as jnp
from jax import lax
from jax.experimental import pallas as pl
from jax.experimental.pallas import tpu as pltpu

RMS_EPS = 1e-6
LANES = 128
SUBLANES = 8
BF16_ROWS = 16
VMEM_LIMIT = 56 * 1024 * 1024
PEER_TOPK = 16
NEG_INF = float("-inf")
INV_SQRT2 = 0.7071067811865476
LOG2E = 1.4426950408889634
NT_DIMS = (((1,), (1,)), ((), ()))


def _cparams(*sem):
    return pltpu.CompilerParams(dimension_semantics=sem, vmem_limit_bytes=VMEM_LIMIT)


def _rmsnorm_kernel(x_ref, g_ref, o_ref):
    x = x_ref[...]
    inv = lax.rsqrt(jnp.mean(x * x, axis=-1, keepdims=True) + RMS_EPS)
    o_ref[...] = (x * inv * g_ref[...]).astype(o_ref.dtype)


def rmsnorm(x, gain, out_dtype, *, tm=512):
    t, d = x.shape
    tm = min(tm, t)
    assert t % tm == 0
    return pl.pallas_call(
        _rmsnorm_kernel,
        grid=(t // tm,),
        in_specs=[pl.BlockSpec((tm, d), lambda i: (i, 0)), pl.BlockSpec((1, d), lambda i: (0, 0))],
        out_specs=pl.BlockSpec((tm, d), lambda i: (i, 0)),
        out_shape=jax.ShapeDtypeStruct((t, d), out_dtype),
        compiler_params=_cparams("parallel"),
        name="rmsnorm",
    )(x, gain.reshape(1, d))


def _matmul_kernel(x_ref, w_ref, o_ref, wb_ref, *, scale):
    @pl.when(pl.program_id(1) == 0)
    def _():
        wb_ref[...] = w_ref[...].astype(jnp.bfloat16)

    acc = jnp.dot(x_ref[...], wb_ref[...], preferred_element_type=jnp.float32)
    if scale != 1.0:
        acc = acc * scale
    o_ref[...] = acc.astype(o_ref.dtype)


def matmul(x, w, col0, ncols, out_dtype, *, scale=1.0, tm=1024, tn=1024):
    t, k = x.shape
    tm = min(tm, t)
    tn = min(tn, ncols)
    assert t % tm == 0 and ncols % tn == 0 and col0 % tn == 0
    jb0 = col0 // tn
    return pl.pallas_call(
        functools.partial(_matmul_kernel, scale=scale),
        grid=(ncols // tn, t // tm),
        in_specs=[
            pl.BlockSpec((tm, k), lambda j, i: (i, 0)),
            pl.BlockSpec((k, tn), lambda j, i: (0, jb0 + j)),
        ],
        out_specs=pl.BlockSpec((tm, tn), lambda j, i: (i, j)),
        out_shape=jax.ShapeDtypeStruct((t, ncols), out_dtype),
        scratch_shapes=[pltpu.VMEM((k, tn), jnp.bfloat16)],
        compiler_params=_cparams("parallel", "arbitrary"),
        name="matmul",
    )(x, w)


def _matmul_nt_kernel(wt_ref, x_ref, o_ref):
    o_ref[0] = lax.dot_general(wt_ref[...], x_ref[...], NT_DIMS,
                               preferred_element_type=jnp.float32).astype(o_ref.dtype)


def matmul_nt(wt, x, out_dtype, *, tm, tn=1024):
    t, k = x.shape
    n = wt.shape[0]
    tn = min(tn, n)
    assert t % tm == 0 and n % tn == 0
    return pl.pallas_call(
        _matmul_nt_kernel,
        grid=(t // tm, n // tn),
        in_specs=[
            pl.BlockSpec((tn, k), lambda i, j: (j, 0)),
            pl.BlockSpec((tm, k), lambda i, j: (i, 0)),
        ],
        out_specs=pl.BlockSpec((1, tn, tm), lambda i, j: (i, j, 0)),
        out_shape=jax.ShapeDtypeStruct((t // tm, n, tm), out_dtype),
        compiler_params=_cparams("parallel", "parallel"),
        name="matmul_nt",
    )(wt, x)


def _conv_kernel(cb_ref, cc_ref, ch_ref, cch_ref, chh_ref, w_ref, o_ref, z_ref):
    tm = cb_ref.shape[0]
    z_halo = cch_ref[...] * chh_ref[...]
    z_ref[0:SUBLANES, :] = jnp.where(pl.program_id(0) == 0, 0.0, z_halo)
    z = cc_ref[...] * ch_ref[...]
    z_ref[SUBLANES:, :] = z
    w = w_ref[...]
    k = w.shape[0]
    y = z * w[k - 1:k, :]
    for j in range(1, k):
        y = y + z_ref[pl.ds(SUBLANES - j, tm), :] * w[k - 1 - j:k - j, :]
    o_ref[...] = (cb_ref[...] * y).astype(o_ref.dtype)


def conv_mixer(pc, conv_w, *, tm=512):
    t, c3 = pc.shape
    c = c3 // 3
    k = conv_w.shape[0]
    assert k - 1 <= SUBLANES
    tm = min(tm, t)
    assert t % tm == 0 and tm % SUBLANES == 0
    hb = tm // SUBLANES

    def halo_map(col):
        return lambda i: (jnp.maximum(i * hb - 1, 0), col)

    return pl.pallas_call(
        _conv_kernel,
        grid=(t // tm,),
        in_specs=[
            pl.BlockSpec((tm, c), lambda i: (i, 0)),
            pl.BlockSpec((tm, c), lambda i: (i, 1)),
            pl.BlockSpec((tm, c), lambda i: (i, 2)),
            pl.BlockSpec((SUBLANES, c), halo_map(1)),
            pl.BlockSpec((SUBLANES, c), halo_map(2)),
            pl.BlockSpec((k, c), lambda i: (0, 0)),
        ],
        out_specs=pl.BlockSpec((tm, c), lambda i: (i, 0)),
        out_shape=jax.ShapeDtypeStruct((t, c), jnp.bfloat16),
        scratch_shapes=[pltpu.VMEM((tm + SUBLANES, c), jnp.float32)],
        compiler_params=_cparams("parallel"),
        name="conv_mixer",
    )(pc, pc, pc, pc, pc, conv_w)


ATT_SUB = 128


def _attn_kernel(q_ref, k_ref, vt_ref, tri_ref, o_ref, acc_ref, *, dh):
    blk = q_ref.shape[0]
    nheads = 2 * (q_ref.shape[1] // LANES)
    nsub = blk // ATT_SUB
    i = pl.program_id(1)
    lane = lax.broadcasted_iota(jnp.int32, (1, LANES), 1)
    head0 = lane < dh
    qh = []
    for g in range(nheads // 2):
        q2 = q_ref[:, g * LANES:(g + 1) * LANES]
        zero = jnp.zeros_like(q2)
        qh += [jnp.where(head0, q2, zero), jnp.where(head0, zero, q2)]
    tri = tri_ref[...]
    acc_ref[...] = jnp.zeros_like(acc_ref)

    def block(jb, cars, diagonal):
        kb = k_ref[pl.ds(pl.multiple_of(jb * blk, blk), blk), :]
        vt = vt_ref[jb]
        new_cars = []
        zts = [lax.dot_general(kb[:, (h // 2) * LANES:(h // 2 + 1) * LANES], qh[h], NT_DIMS,
                               preferred_element_type=jnp.float32)
               for h in range(nheads)]
        for h in range(nheads):
            car = cars[h]
            ws = [None] * nsub
            for s in range(nsub - 1, -1, -1):
                z = zts[h][s * ATT_SUB:(s + 1) * ATT_SUB]
                sp = jnp.maximum(z, 0.0) + jnp.log(1.0 + jnp.exp2(jnp.abs(z) * -LOG2E))
                logsig = z - sp
                if diagonal:
                    key = lax.broadcasted_iota(jnp.int32, z.shape, 0) + s * ATT_SUB
                    qry = lax.broadcasted_iota(jnp.int32, z.shape, 1)
                    mask = key < qry
                    sp = jnp.where(mask, sp, 0.0)
                r = jnp.dot(tri, sp.astype(jnp.bfloat16), preferred_element_type=jnp.float32)
                sfx = r[:ATT_SUB]
                tot = r[ATT_SUB:ATT_SUB + 1]
                w = jnp.exp(logsig - sfx - car)
                if diagonal:
                    w = jnp.where(mask, w, 0.0)
                ws[s] = w.astype(jnp.bfloat16)
                car = car + tot
            wt = jnp.concatenate(ws, axis=0)
            acc_ref[h * dh:(h + 1) * dh, :] += jnp.dot(vt[h * dh:(h + 1) * dh, :], wt,
                                                       preferred_element_type=jnp.float32)
            new_cars.append(car)
        return tuple(new_cars)

    car0 = jnp.zeros((1, blk), jnp.float32)
    cars = block(i, (car0,) * nheads, True)

    def body(n, cars):
        return block(i - 1 - n, cars, False)

    lax.fori_loop(0, i, body, cars)
    o_ref[...] = acc_ref[...].T.astype(o_ref.dtype)


def _tri_matrix():
    s = jnp.arange(ATT_SUB)[:, None]
    j = jnp.arange(ATT_SUB)[None, :]
    later = (j > s).astype(jnp.bfloat16)
    ones = jnp.ones((BF16_ROWS, ATT_SUB), jnp.bfloat16)
    return jnp.concatenate([later, ones], axis=0)


def attention(q, k, vt, n_heads, dh, *, blk=512, heads_per_step=4):
    t = q.shape[0]
    assert 2 * dh == LANES and heads_per_step % 2 == 0 and dh % BF16_ROWS == 0
    assert n_heads % heads_per_step == 0
    assert t % blk == 0 and blk % ATT_SUB == 0 and vt.shape == (t // blk, n_heads * dh, blk)
    ngroup = n_heads // heads_per_step
    gw = heads_per_step * dh
    nb = t // blk
    return pl.pallas_call(
        functools.partial(_attn_kernel, dh=dh),
        grid=(ngroup, nb),
        in_specs=[
            pl.BlockSpec((blk, gw), lambda p, i: (i, p)),
            pl.BlockSpec((t, gw), lambda p, i: (0, p)),
            pl.BlockSpec((nb, gw, blk), lambda p, i: (0, p, 0)),
            pl.BlockSpec((ATT_SUB + BF16_ROWS, ATT_SUB), lambda p, i: (0, 0)),
        ],
        out_specs=pl.BlockSpec((blk, gw), lambda p, i: (i, p)),
        out_shape=jax.ShapeDtypeStruct((t, n_heads * dh), jnp.bfloat16),
        scratch_shapes=[pltpu.VMEM((gw, blk), jnp.float32)],
        compiler_params=_cparams("parallel", "parallel"),
        name="stickbreak_attention",
    )(q, k, vt, _tri_matrix())


def _merge_kernel(yc_ref, ya_ref, wc_ref, wa_ref, gc_ref, ga_ref, o_ref):
    yc = jnp.dot(yc_ref[...], wc_ref[...], preferred_element_type=jnp.float32)
    ya = jnp.dot(ya_ref[...], wa_ref[...], preferred_element_type=jnp.float32)
    o_ref[...] = (jax.nn.sigmoid(gc_ref[...]) * yc + jax.nn.sigmoid(ga_ref[...]) * ya).astype(o_ref.dtype)


def merge(yc, ya, wc, wa, gates, *, tm=1024, tn=1024):
    t, c = yc.shape
    a = ya.shape[1]
    d = wc.shape[1]
    tm = min(tm, t)
    tn = min(tn, d)
    assert t % tm == 0 and d % tn == 0
    nj = d // tn
    return pl.pallas_call(
        _merge_kernel,
        grid=(t // tm, nj),
        in_specs=[
            pl.BlockSpec((tm, c), lambda i, j: (i, 0)),
            pl.BlockSpec((tm, a), lambda i, j: (i, 0)),
            pl.BlockSpec((c, tn), lambda i, j: (0, j)),
            pl.BlockSpec((a, tn), lambda i, j: (0, j)),
            pl.BlockSpec((tm, tn), lambda i, j: (i, j)),
            pl.BlockSpec((tm, tn), lambda i, j: (i, nj + j)),
        ],
        out_specs=pl.BlockSpec((tm, tn), lambda i, j: (i, j)),
        out_shape=jax.ShapeDtypeStruct((t, d), jnp.bfloat16),
        compiler_params=_cparams("parallel", "parallel"),
        name="merge_mixers",
    )(yc, ya, wc, wa, gates, gates)


def _out_proj_kernel(m_ref, w_ref, x_ref, o_ref):
    o_ref[...] = x_ref[...] + jnp.dot(m_ref[...], w_ref[...], preferred_element_type=jnp.float32)


def out_proj(merged, w, x, *, tm=1024, tn=1024):
    t, d = merged.shape
    n = w.shape[1]
    tm = min(tm, t)
    tn = min(tn, n)
    assert t % tm == 0 and n % tn == 0
    return pl.pallas_call(
        _out_proj_kernel,
        grid=(t // tm, n // tn),
        in_specs=[
            pl.BlockSpec((tm, d), lambda i, j: (i, 0)),
            pl.BlockSpec((d, tn), lambda i, j: (0, j)),
            pl.BlockSpec((tm, tn), lambda i, j: (i, j)),
        ],
        out_specs=pl.BlockSpec((tm, tn), lambda i, j: (i, j)),
        out_shape=jax.ShapeDtypeStruct((t, n), jnp.float32),
        compiler_params=_cparams("parallel", "parallel"),
        name="out_proj",
    )(merged, w, x)


def _scores_kernel(x_ref, w_ref, k_ref, o_ref, wb_ref):
    @pl.when(pl.program_id(1) == 0)
    def _():
        wb_ref[...] = w_ref[...].astype(jnp.bfloat16)

    half = k_ref.shape[2]
    q = jnp.dot(x_ref[...], wb_ref[...], preferred_element_type=jnp.float32)
    for g in range(o_ref.shape[0]):
        qg = q[:, g * half:(g + 1) * half].astype(jnp.bfloat16)
        keys = k_ref[g % 2]
        o_ref[g] = lax.dot_general(keys, qg, NT_DIMS, preferred_element_type=jnp.float32)


def peer_scores(xn, wq, subkeys, n_heads, *, tm=1024, tn=1024):
    t, d = xn.shape
    _, nk, half = subkeys.shape
    qd = wq.shape[1]
    assert qd == n_heads * 2 * half
    tm = min(tm, t)
    tn = min(tn, qd)
    assert t % tm == 0 and qd % tn == 0 and tn % (2 * half) == 0
    gpt = tn // half
    return pl.pallas_call(
        _scores_kernel,
        grid=(qd // tn, t // tm),
        in_specs=[
            pl.BlockSpec((tm, d), lambda j, i: (i, 0)),
            pl.BlockSpec((d, tn), lambda j, i: (0, j)),
            pl.BlockSpec((2, nk, half), lambda j, i: (0, 0, 0)),
        ],
        out_specs=pl.BlockSpec((gpt, nk, tm), lambda j, i: (j, 0, i)),
        out_shape=jax.ShapeDtypeStruct((2 * n_heads, nk, t), jnp.float32),
        scratch_shapes=[pltpu.VMEM((d, tn), jnp.bfloat16)],
        compiler_params=_cparams("parallel", "arbitrary"),
        name="peer_scores",
    )(xn, wq, subkeys)


def _row_ids(rows):
    return lax.broadcasted_iota(jnp.int32, (rows, LANES), 0).astype(jnp.float32)


def _extract_top(cur_ref, count):
    rows, width = cur_ref.shape
    ntile = width // LANES
    row = _row_ids(rows)
    slot = _row_ids(count)

    def body(r, carry):
        vals, firsts = carry
        rf = lax.convert_element_type(r, jnp.float32)
        new_vals, new_firsts = [], []
        for tile in range(ntile):
            sl = slice(tile * LANES, (tile + 1) * LANES)
            cur = cur_ref[:, sl]
            m = jnp.max(cur, axis=0, keepdims=True)
            first = jnp.min(jnp.where(cur == m, row, float(rows)), axis=0, keepdims=True)
            cur_ref[:, sl] = jnp.where(row == first, NEG_INF, cur)
            new_vals.append(jnp.where(slot == rf, m, vals[:, sl]))
            new_firsts.append(jnp.where(slot == rf, first, firsts[:, sl]))
        return jnp.concatenate(new_vals, axis=1), jnp.concatenate(new_firsts, axis=1)

    init = jnp.zeros((count, width), jnp.float32)
    return lax.fori_loop(0, count, body, (init, init))


def _topk_kernel(s_ref, r2_ref, e2_ref, ell_ref, w_ref, cur_ref, cand_ref):
    k = PEER_TOPK
    nchunk = r2_ref.shape[1]
    tl = nchunk * LANES
    cur_ref[:, :tl] = s_ref[0]
    cur_ref[:, tl:] = s_ref[1]
    vals, firsts = _extract_top(cur_ref, k)
    a = vals[:, :tl]
    b = vals[:, tl:]
    sub = lax.broadcasted_iota(jnp.int32, (SUBLANES, tl), 0)
    cand_ref[0:SUBLANES] = a[0:1] + b[0:SUBLANES]
    cand_ref[SUBLANES:2 * SUBLANES] = a[0:1] + b[SUBLANES:2 * SUBLANES]
    for i in range(1, SUBLANES):
        cand_ref[(i + 1) * SUBLANES:(i + 2) * SUBLANES] = jnp.where(
            sub < k // (i + 1), a[i:i + 1] + b[0:SUBLANES], NEG_INF)
    base = (SUBLANES + 1) * SUBLANES
    cand_ref[base:base + SUBLANES] = a[SUBLANES:] + b[0:1]
    cvals, cfirsts = _extract_top(cand_ref, k)
    z = jnp.sum(jnp.exp(cvals - cvals[0:1]), axis=0, keepdims=True)
    pair_i = jnp.where(cfirsts < float(2 * SUBLANES), 0.0,
                       jnp.where(cfirsts < float(base), jnp.floor(cfirsts * (1.0 / SUBLANES)) - 1.0,
                                 cfirsts - float(base - SUBLANES)))
    lims = [jnp.sum(jnp.where(pair_i == float(i), 1.0, 0.0), axis=0, keepdims=True)
            for i in range(k)]
    e2 = jnp.exp(s_ref[1] - b[0:1])
    w = jnp.exp(s_ref[0] - a[0:1]) * (0.5 / z)
    row = _row_ids(s_ref.shape[1])
    for c in range(nchunk):
        sl = slice(c * LANES, (c + 1) * LANES)
        sl2 = slice(tl + c * LANES, tl + (c + 1) * LANES)
        r2 = jnp.full(row.shape, float(k), jnp.float32)
        ell = jnp.zeros(row.shape, jnp.float32)
        for r in range(k):
            r2 = jnp.where(row == firsts[r:r + 1, sl2], float(r), r2)
            ell = jnp.where(row == firsts[r:r + 1, sl], lims[r][:, sl], ell)
        r2_ref[0, c] = r2
        e2_ref[0, c] = e2[:, sl]
        ell_ref[0, c] = ell
        w_ref[0, c] = w[:, sl]


def peer_topk(scores, n_heads, *, tl=512):
    hs, nk, t = scores.shape
    assert hs == 2 * n_heads and PEER_TOPK == 2 * SUBLANES
    tl = min(tl, t)
    assert t % tl == 0 and tl % LANES == 0
    nchunk = tl // LANES
    ncand = (SUBLANES + 2) * SUBLANES
    shape = (n_heads, t // LANES, nk, LANES)
    ospec = pl.BlockSpec((1, nchunk, nk, LANES), lambda h, i: (h, i, 0, 0))
    return pl.pallas_call(
        _topk_kernel,
        grid=(n_heads, t // tl),
        in_specs=[pl.BlockSpec((2, nk, tl), lambda h, i: (h, 0, i))],
        out_specs=[ospec] * 4,
        out_shape=[jax.ShapeDtypeStruct(shape, jnp.float32)] * 4,
        scratch_shapes=[pltpu.VMEM((nk, 2 * tl), jnp.float32), pltpu.VMEM((ncand, tl), jnp.float32)],
        compiler_params=_cparams("parallel", "parallel"),
        name="peer_topk",
    )(scores)


def _peer_hidden_kernel(u_ref, xn_ref, o_ref, ub_ref):
    @pl.when(pl.program_id(1) == 0)
    def _():
        ub_ref[...] = u_ref[...].astype(jnp.bfloat16)

    hid = lax.dot_general(ub_ref[...], xn_ref[...], NT_DIMS, preferred_element_type=jnp.float32)
    o_ref[...] = hid * (1.0 + lax.erf(hid * INV_SQRT2))


def peer_hidden(xn, u, *, tb=1024, eb=1024):
    t, d = xn.shape
    e = u.shape[0]
    tb = min(tb, t)
    assert t % tb == 0 and e % eb == 0
    return pl.pallas_call(
        _peer_hidden_kernel,
        grid=(e // eb, t // tb),
        in_specs=[
            pl.BlockSpec((eb, d), lambda j, i: (j, 0)),
            pl.BlockSpec((tb, d), lambda j, i: (i, 0)),
        ],
        out_specs=pl.BlockSpec((eb, tb), lambda j, i: (j, i)),
        out_shape=jax.ShapeDtypeStruct((e, t), jnp.float32),
        scratch_shapes=[pltpu.VMEM((eb, d), jnp.bfloat16)],
        compiler_params=_cparams("parallel", "arbitrary"),
        name="peer_hidden",
    )(u, xn)


GATE_GROUP = 4


def _peer_dense_kernel(h_ref, vt_ref, r2_ref, e2_ref, ell_ref, w_ref, x_ref, gain_ref, o_ref,
                       acc_ref, a_ref, *, final_norm):
    n_heads, nchunk, nk, _ = r2_ref.shape
    na = ell_ref.shape[2]
    j = pl.program_id(1)

    @pl.when(j == 0)
    def _():
        acc_ref[...] = jnp.zeros_like(acc_ref)

    hb = nk // 2
    bf = jnp.bfloat16
    for c in range(nchunk):
        lanes = slice(c * LANES, (c + 1) * LANES)
        for half in range(2):
            keys = slice(half * hb, (half + 1) * hb)
            for a0 in range(0, na, GATE_GROUP):
                group = range(a0, a0 + GATE_GROUP)
                g = {ai: None for ai in group}
                for h in range(n_heads):
                    r2 = r2_ref[h, c, keys, :].astype(bf)
                    e2 = e2_ref[h, c, keys, :].astype(bf)
                    for ai in group:
                        lim = jnp.broadcast_to(ell_ref[h, c, ai:ai + 1, :], (hb, LANES)).astype(bf)
                        wa = jnp.broadcast_to(w_ref[h, c, ai:ai + 1, :], (hb, LANES)).astype(bf)
                        term = jnp.where(r2 < lim, e2, jnp.zeros_like(e2)) * wa
                        g[ai] = term if g[ai] is None else g[ai] + term
                for ai in group:
                    rows = slice(ai * nk + half * hb, ai * nk + (half + 1) * hb)
                    act = h_ref[rows, lanes]
                    a_ref[rows, lanes] = act.astype(bf) * g[ai]
    acc_ref[...] += jnp.dot(vt_ref[...], a_ref[...], preferred_element_type=jnp.float32)

    @pl.when(j == pl.num_programs(1) - 1)
    def _():
        x = x_ref[...] + acc_ref[...].T
        if final_norm:
            inv = lax.rsqrt(jnp.mean(x * x, axis=-1, keepdims=True) + RMS_EPS)
            x = x * inv * gain_ref[...]
        o_ref[...] = x


def peer_dense(act_t, vt, r2, e2, ell, w, x, gain, *, final_norm, tb=512, na=8):
    e, t = act_t.shape
    d = vt.shape[0]
    n_heads, _, nk, _ = r2.shape
    eb = na * nk
    tb = min(tb, t)
    assert t % tb == 0 and tb % LANES == 0 and e % eb == 0 and e == nk * nk and na % SUBLANES == 0
    nchunk = tb // LANES
    key_spec = pl.BlockSpec((n_heads, nchunk, nk, LANES), lambda i, j: (0, i, 0, 0))
    row_spec = pl.BlockSpec((n_heads, nchunk, na, LANES), lambda i, j: (0, i, j, 0))
    return pl.pallas_call(
        functools.partial(_peer_dense_kernel, final_norm=final_norm),
        grid=(t // tb, e // eb),
        in_specs=[
            pl.BlockSpec((eb, tb), lambda i, j: (j, i)),
            pl.BlockSpec((d, eb), lambda i, j: (0, j)),
            key_spec, key_spec, row_spec, row_spec,
            pl.BlockSpec((tb, d), lambda i, j: (i, 0)),
            pl.BlockSpec((1, d), lambda i, j: (0, 0)),
        ],
        out_specs=pl.BlockSpec((tb, d), lambda i, j: (i, 0)),
        out_shape=jax.ShapeDtypeStruct((t, d), jnp.float32),
        scratch_shapes=[pltpu.VMEM((d, tb), jnp.float32), pltpu.VMEM((eb, tb), jnp.bfloat16)],
        compiler_params=_cparams("parallel", "arbitrary"),
        name="peer_dense",
    )(act_t, vt, r2, e2, ell, w, x, gain.reshape(1, d))


ATT_BLK = 512


def _layer(x, norm_mix, w_in, conv_w, w_conv_out, w_attn_out, w_out, norm_ffn,
           peer_wq, peer_subkeys, peer_u, peer_v, norm_final, is_last):
    t, d = x.shape
    c = conv_w.shape[1]
    a = w_attn_out.shape[0]
    n_keys, half = peer_subkeys.shape[1:]
    peer_heads = peer_wq.shape[1] // (2 * half)
    dh = LANES // 2
    attn_heads = a // dh
    bf = jnp.bfloat16

    xn = rmsnorm(x, norm_mix, bf)
    pc = matmul(xn, w_in, 0, 3 * c, jnp.float32)
    q = matmul(xn, w_in, 3 * c, a, bf, scale=dh ** -0.5)
    k = matmul(xn, w_in, 3 * c + a, a, bf)
    vt = matmul_nt(w_in[:, 3 * c + 2 * a:3 * c + 3 * a].T.astype(bf), xn, bf, tm=min(ATT_BLK, t))
    gates = matmul(xn, w_in, 3 * c + 3 * a, 2 * d, jnp.float32)

    yc = conv_mixer(pc, conv_w)
    ya = attention(q, k, vt, attn_heads, dh, blk=min(ATT_BLK, t))
    merged = merge(yc, ya, w_conv_out.astype(bf), w_attn_out.astype(bf), gates)
    x1 = out_proj(merged, w_out.astype(bf), x)

    xn2 = rmsnorm(x1, norm_ffn, bf)
    scores = peer_scores(xn2, peer_wq, peer_subkeys.astype(bf), peer_heads)
    r2, e2, ell, w = peer_topk(scores, peer_heads)
    act_t = peer_hidden(xn2, peer_u)
    return peer_dense(act_t, peer_v.T.astype(bf), r2, e2, ell, w, x1, norm_final, final_norm=is_last)


def kernel(x, norm_mix, w_in, conv_w, w_conv_out, w_attn_out, w_out, norm_ffn, peer_wq,
           peer_subkeys, peer_u, peer_v, norm_final):
    b, t, d = x.shape
    depth = norm_mix.shape[0]
    outs = []
    for bi in range(b):
        xb = x[bi]
        for layer in range(depth):
            xb = _layer(xb, norm_mix[layer], w_in[layer], conv_w[layer], w_conv_out[layer],
                        w_attn_out[layer], w_out[layer], norm_ffn[layer], peer_wq[layer],
                        peer_subkeys[layer], peer_u[layer], peer_v[layer], norm_final,
                        is_last=layer == depth - 1)
        outs.append(xb)
    return jnp.stack(outs, axis=0)
```

```python
import functools

import jax
import jax.numpy as jnp
from jax import lax
from jax.experimental import pallas as pl
from jax.experimental.pallas import tpu as pltpu

RMS_EPS = 1e-6
LANES = 128
SUBLANES = 8
BF16_ROWS = 16
VMEM_LIMIT = 56 * 1024 * 1024
PEER_TOPK = 16
NEG_INF = float("-inf")
INV_SQRT2 = 0.7071067811865476
LOG2E = 1.4426950408889634
NT_DIMS = (((1,), (1,)), ((), ()))


def _cparams(*sem):
    return pltpu.CompilerParams(dimension_semantics=sem, vmem_limit_bytes=VMEM_LIMIT)


def _rmsnorm_kernel(x_ref, g_ref, o_ref):
    x = x_ref[...]
    inv = lax.rsqrt(jnp.mean(x * x, axis=-1, keepdims=True) + RMS_EPS)
    o_ref[...] = (x * inv * g_ref[...]).astype(o_ref.dtype)


def rmsnorm(x, gain, out_dtype, *, tm=512):
    t, d = x.shape
    tm = min(tm, t)
    assert t % tm == 0
    return pl.pallas_call(
        _rmsnorm_kernel,
        grid=(t // tm,),
        in_specs=[pl.BlockSpec((tm, d), lambda i: (i, 0)), pl.BlockSpec((1, d), lambda i: (0, 0))],
        out_specs=pl.BlockSpec((tm, d), lambda i: (i, 0)),
        out_shape=jax.ShapeDtypeStruct((t, d), out_dtype),
        compiler_params=_cparams("parallel"),
        name="rmsnorm",
    )(x, gain.reshape(1, d))


def _matmul_kernel(x_ref, w_ref, o_ref, wb_ref, *, scale):
    @pl.when(pl.program_id(1) == 0)
    def _():
        wb_ref[...] = w_ref[...].astype(jnp.bfloat16)

    acc = jnp.dot(x_ref[...], wb_ref[...], preferred_element_type=jnp.float32)
    if scale != 1.0:
        acc = acc * scale
    o_ref[...] = acc.astype(o_ref.dtype)


def matmul(x, w, col0, ncols, out_dtype, *, scale=1.0, tm=1024, tn=1024):
    t, k = x.shape
    tm = min(tm, t)
    tn = min(tn, ncols)
    assert t % tm == 0 and ncols % tn == 0 and col0 % tn == 0
    jb0 = col0 // tn
    return pl.pallas_call(
        functools.partial(_matmul_kernel, scale=scale),
        grid=(ncols // tn, t // tm),
        in_specs=[
            pl.BlockSpec((tm, k), lambda j, i: (i, 0)),
            pl.BlockSpec((k, tn), lambda j, i: (0, jb0 + j)),
        ],
        out_specs=pl.BlockSpec((tm, tn), lambda j, i: (i, j)),
        out_shape=jax.ShapeDtypeStruct((t, ncols), out_dtype),
        scratch_shapes=[pltpu.VMEM((k, tn), jnp.bfloat16)],
        compiler_params=_cparams("parallel", "arbitrary"),
        name="matmul",
    )(x, w)


def _matmul_nt_kernel(wt_ref, x_ref, o_ref):
    o_ref[0] = lax.dot_general(wt_ref[...], x_ref[...], NT_DIMS,
                               preferred_element_type=jnp.float32).astype(o_ref.dtype)


def matmul_nt(wt, x, out_dtype, *, tm, tn=1024):
    t, k = x.shape
    n = wt.shape[0]
    tn = min(tn, n)
    assert t % tm == 0 and n % tn == 0
    return pl.pallas_call(
        _matmul_nt_kernel,
        grid=(t // tm, n // tn),
        in_specs=[
            pl.BlockSpec((tn, k), lambda i, j: (j, 0)),
            pl.BlockSpec((tm, k), lambda i, j: (i, 0)),
        ],
        out_specs=pl.BlockSpec((1, tn, tm), lambda i, j: (i, j, 0)),
        out_shape=jax.ShapeDtypeStruct((t // tm, n, tm), out_dtype),
        compiler_params=_cparams("parallel", "parallel"),
        name="matmul_nt",
    )(wt, x)


def _conv_kernel(cb_ref, cc_ref, ch_ref, cch_ref, chh_ref, w_ref, o_ref, z_ref):
    tm = cb_ref.shape[0]
    z_halo = cch_ref[...] * chh_ref[...]
    z_ref[0:SUBLANES, :] = jnp.where(pl.program_id(0) == 0, 0.0, z_halo)
    z = cc_ref[...] * ch_ref[...]
    z_ref[SUBLANES:, :] = z
    w = w_ref[...]
    k = w.shape[0]
    y = z * w[k - 1:k, :]
    for j in range(1, k):
        y = y + z_ref[pl.ds(SUBLANES - j, tm), :] * w[k - 1 - j:k - j, :]
    o_ref[...] = (cb_ref[...] * y).astype(o_ref.dtype)


def conv_mixer(pc, conv_w, *, tm=512):
    t, c3 = pc.shape
    c = c3 // 3
    k = conv_w.shape[0]
    assert k - 1 <= SUBLANES
    tm = min(tm, t)
    assert t % tm == 0 and tm % SUBLANES == 0
    hb = tm // SUBLANES

    def halo_map(col):
        return lambda i: (jnp.maximum(i * hb - 1, 0), col)

    return pl.pallas_call(
        _conv_kernel,
        grid=(t // tm,),
        in_specs=[
            pl.BlockSpec((tm, c), lambda i: (i, 0)),
            pl.BlockSpec((tm, c), lambda i: (i, 1)),
            pl.BlockSpec((tm, c), lambda i: (i, 2)),
            pl.BlockSpec((SUBLANES, c), halo_map(1)),
            pl.BlockSpec((SUBLANES, c), halo_map(2)),
            pl.BlockSpec((k, c), lambda i: (0, 0)),
        ],
        out_specs=pl.BlockSpec((tm, c), lambda i: (i, 0)),
        out_shape=jax.ShapeDtypeStruct((t, c), jnp.bfloat16),
        scratch_shapes=[pltpu.VMEM((tm + SUBLANES, c), jnp.float32)],
        compiler_params=_cparams("parallel"),
        name="conv_mixer",
    )(pc, pc, pc, pc, pc, conv_w)


ATT_SUB = 128


def _attn_kernel(q_ref, k_ref, vt_ref, tri_ref, o_ref, acc_ref, *, dh):
    blk = q_ref.shape[0]
    nheads = 2 * (q_ref.shape[1] // LANES)
    nsub = blk // ATT_SUB
    i = pl.program_id(1)
    lane = lax.broadcasted_iota(jnp.int32, (1, LANES), 1)
    head0 = lane < dh
    qh = []
    for g in range(nheads // 2):
        q2 = q_ref[:, g * LANES:(g + 1) * LANES]
        zero = jnp.zeros_like(q2)
        qh += [jnp.where(head0, q2, zero), jnp.where(head0, zero, q2)]
    tri = tri_ref[...]
    acc_ref[...] = jnp.zeros_like(acc_ref)

    def block(jb, cars, diagonal):
        kb = k_ref[pl.ds(pl.multiple_of(jb * blk, blk), blk), :]
        vt = vt_ref[jb]
        new_cars = []
        zts = [lax.dot_general(kb[:, (h // 2) * LANES:(h // 2 + 1) * LANES], qh[h], NT_DIMS,
                               preferred_element_type=jnp.float32)
               for h in range(nheads)]
        for h in range(nheads):
            car = cars[h]
            ws = [None] * nsub
            for s in range(nsub - 1, -1, -1):
                z = zts[h][s * ATT_SUB:(s + 1) * ATT_SUB]
                sp = jnp.maximum(z, 0.0) + jnp.log(1.0 + jnp.exp2(jnp.abs(z) * -LOG2E))
                logsig = z - sp
                if diagonal:
                    key = lax.broadcasted_iota(jnp.int32, z.shape, 0) + s * ATT_SUB
                    qry = lax.broadcasted_iota(jnp.int32, z.shape, 1)
                    mask = key < qry
                    sp = jnp.where(mask, sp, 0.0)
                r = jnp.dot(tri, sp.astype(jnp.bfloat16), preferred_element_type=jnp.float32)
                sfx = r[:ATT_SUB]
                tot = r[ATT_SUB:ATT_SUB + 1]
                w = jnp.exp(logsig - sfx - car)
                if diagonal:
                    w = jnp.where(mask, w, 0.0)
                ws[s] = w.astype(jnp.bfloat16)
                car = car + tot
            wt = jnp.concatenate(ws, axis=0)
            acc_ref[h * dh:(h + 1) * dh, :] += jnp.dot(vt[h * dh:(h + 1) * dh, :], wt,
                                                       preferred_element_type=jnp.float32)
            new_cars.append(car)
        return tuple(new_cars)

    car0 = jnp.zeros((1, blk), jnp.float32)
    cars = block(i, (car0,) * nheads, True)

    def body(n, cars):
        return block(i - 1 - n, cars, False)

    lax.fori_loop(0, i, body, cars)
    o_ref[...] = acc_ref[...].T.astype(o_ref.dtype)


def _tri_matrix():
    s = jnp.arange(ATT_SUB)[:, None]
    j = jnp.arange(ATT_SUB)[None, :]
    later = (j > s).astype(jnp.bfloat16)
    ones = jnp.ones((BF16_ROWS, ATT_SUB), jnp.bfloat16)
    return jnp.concatenate([later, ones], axis=0)


def attention(q, k, vt, n_heads, dh, *, blk=512, heads_per_step=8):
    t = q.shape[0]
    heads_per_step = min(heads_per_step, n_heads)
    assert 2 * dh == LANES and heads_per_step % 2 == 0 and dh % BF16_ROWS == 0
    assert n_heads % heads_per_step == 0
    assert t % blk == 0 and blk % ATT_SUB == 0 and vt.shape == (t // blk, n_heads * dh, blk)
    ngroup = n_heads // heads_per_step
    gw = heads_per_step * dh
    nb = t // blk
    return pl.pallas_call(
        functools.partial(_attn_kernel, dh=dh),
        grid=(ngroup, nb),
        in_specs=[
            pl.BlockSpec((blk, gw), lambda p, i: (i, p)),
            pl.BlockSpec((t, gw), lambda p, i: (0, p), pipeline_mode=pl.Buffered(1)),
            pl.BlockSpec((nb, gw, blk), lambda p, i: (0, p, 0), pipeline_mode=pl.Buffered(1)),
            pl.BlockSpec((ATT_SUB + BF16_ROWS, ATT_SUB), lambda p, i: (0, 0)),
        ],
        out_specs=pl.BlockSpec((blk, gw), lambda p, i: (i, p)),
        out_shape=jax.ShapeDtypeStruct((t, n_heads * dh), jnp.bfloat16),
        scratch_shapes=[pltpu.VMEM((gw, blk), jnp.float32)],
        compiler_params=_cparams("parallel", "parallel"),
        name="stickbreak_attention",
    )(q, k, vt, _tri_matrix())


def _merge_kernel(yc_ref, ya_ref, wc_ref, wa_ref, gc_ref, ga_ref, o_ref):
    yc = jnp.dot(yc_ref[...], wc_ref[...], preferred_element_type=jnp.float32)
    ya = jnp.dot(ya_ref[...], wa_ref[...], preferred_element_type=jnp.float32)
    o_ref[...] = (jax.nn.sigmoid(gc_ref[...]) * yc + jax.nn.sigmoid(ga_ref[...]) * ya).astype(o_ref.dtype)


def merge(yc, ya, wc, wa, gates, *, tm=1024, tn=1024):
    t, c = yc.shape
    a = ya.shape[1]
    d = wc.shape[1]
    tm = min(tm, t)
    tn = min(tn, d)
    assert t % tm == 0 and d % tn == 0
    nj = d // tn
    return pl.pallas_call(
        _merge_kernel,
        grid=(t // tm, nj),
        in_specs=[
            pl.BlockSpec((tm, c), lambda i, j: (i, 0)),
            pl.BlockSpec((tm, a), lambda i, j: (i, 0)),
            pl.BlockSpec((c, tn), lambda i, j: (0, j)),
            pl.BlockSpec((a, tn), lambda i, j: (0, j)),
            pl.BlockSpec((tm, tn), lambda i, j: (i, j)),
            pl.BlockSpec((tm, tn), lambda i, j: (i, nj + j)),
        ],
        out_specs=pl.BlockSpec((tm, tn), lambda i, j: (i, j)),
        out_shape=jax.ShapeDtypeStruct((t, d), jnp.bfloat16),
        compiler_params=_cparams("parallel", "parallel"),
        name="merge_mixers",
    )(yc, ya, wc, wa, gates, gates)


def _out_proj_kernel(m_ref, w_ref, x_ref, o_ref):
    o_ref[...] = x_ref[...] + jnp.dot(m_ref[...], w_ref[...], preferred_element_type=jnp.float32)


def out_proj(merged, w, x, *, tm=1024, tn=1024):
    t, d = merged.shape
    n = w.shape[1]
    tm = min(tm, t)
    tn = min(tn, n)
    assert t % tm == 0 and n % tn == 0
    return pl.pallas_call(
        _out_proj_kernel,
        grid=(t // tm, n // tn),
        in_specs=[
            pl.BlockSpec((tm, d), lambda i, j: (i, 0)),
            pl.BlockSpec((d, tn), lambda i, j: (0, j)),
            pl.BlockSpec((tm, tn), lambda i, j: (i, j)),
        ],
        out_specs=pl.BlockSpec((tm, tn), lambda i, j: (i, j)),
        out_shape=jax.ShapeDtypeStruct((t, n), jnp.float32),
        compiler_params=_cparams("parallel", "parallel"),
        name="out_proj",
    )(merged, w, x)


def _scores_kernel(x_ref, w_ref, k_ref, o_ref, wb_ref):
    @pl.when(pl.program_id(1) == 0)
    def _():
        wb_ref[...] = w_ref[...].astype(jnp.bfloat16)

    half = k_ref.shape[2]
    q = jnp.dot(x_ref[...], wb_ref[...], preferred_element_type=jnp.float32)
    for g in range(o_ref.shape[0]):
        qg = q[:, g * half:(g + 1) * half].astype(jnp.bfloat16)
        keys = k_ref[g % 2]
        o_ref[g] = lax.dot_general(keys, qg, NT_DIMS, preferred_element_type=jnp.float32)


def peer_scores(xn, wq, subkeys, n_heads, *, tm=1024, tn=1024):
    t, d = xn.shape
    _, nk, half = subkeys.shape
    qd = wq.shape[1]
    assert qd == n_heads * 2 * half
    tm = min(tm, t)
    tn = min(tn, qd)
    assert t % tm == 0 and qd % tn == 0 and tn % (2 * half) == 0
    gpt = tn // half
    return pl.pallas_call(
        _scores_kernel,
        grid=(qd // tn, t // tm),
        in_specs=[
            pl.BlockSpec((tm, d), lambda j, i: (i, 0)),
            pl.BlockSpec((d, tn), lambda j, i: (0, j)),
            pl.BlockSpec((2, nk, half), lambda j, i: (0, 0, 0)),
        ],
        out_specs=pl.BlockSpec((gpt, nk, tm), lambda j, i: (j, 0, i)),
        out_shape=jax.ShapeDtypeStruct((2 * n_heads, nk, t), jnp.float32),
        scratch_shapes=[pltpu.VMEM((d, tn), jnp.bfloat16)],
        compiler_params=_cparams("parallel", "arbitrary"),
        name="peer_scores",
    )(xn, wq, subkeys)


def _row_ids(rows):
    return lax.broadcasted_iota(jnp.int32, (rows, LANES), 0).astype(jnp.float32)


def _extract_top(cur_ref, count):
    rows, width = cur_ref.shape
    ntile = width // LANES
    row = _row_ids(rows)
    slot = _row_ids(count)

    def body(r, carry):
        vals, firsts = carry
        rf = lax.convert_element_type(r, jnp.float32)
        new_vals, new_firsts = [], []
        for tile in range(ntile):
            sl = slice(tile * LANES, (tile + 1) * LANES)
            cur = cur_ref[:, sl]
            m = jnp.max(cur, axis=0, keepdims=True)
            first = jnp.min(jnp.where(cur == m, row, float(rows)), axis=0, keepdims=True)
            cur_ref[:, sl] = jnp.where(row == first, NEG_INF, cur)
            new_vals.append(jnp.where(slot == rf, m, vals[:, sl]))
            new_firsts.append(jnp.where(slot == rf, first, firsts[:, sl]))
        return jnp.concatenate(new_vals, axis=1), jnp.concatenate(new_firsts, axis=1)

    init = jnp.zeros((count, width), jnp.float32)
    return lax.fori_loop(0, count, body, (init, init))


def _topk_kernel(s_ref, r2_ref, e2_ref, ell_ref, w_ref, cur_ref, cand_ref):
    k = PEER_TOPK
    nchunk = r2_ref.shape[1]
    tl = nchunk * LANES
    cur_ref[:, :tl] = s_ref[0]
    cur_ref[:, tl:] = s_ref[1]
    vals, firsts = _extract_top(cur_ref, k)
    a = vals[:, :tl]
    b = vals[:, tl:]
    sub = lax.broadcasted_iota(jnp.int32, (SUBLANES, tl), 0)
    cand_ref[0:SUBLANES] = a[0:1] + b[0:SUBLANES]
    cand_ref[SUBLANES:2 * SUBLANES] = a[0:1] + b[SUBLANES:2 * SUBLANES]
    for i in range(1, SUBLANES):
        cand_ref[(i + 1) * SUBLANES:(i + 2) * SUBLANES] = jnp.where(
            sub < k // (i + 1), a[i:i + 1] + b[0:SUBLANES], NEG_INF)
    base = (SUBLANES + 1) * SUBLANES
    cand_ref[base:base + SUBLANES] = a[SUBLANES:] + b[0:1]
    cvals, cfirsts = _extract_top(cand_ref, k)
    z = jnp.sum(jnp.exp(cvals - cvals[0:1]), axis=0, keepdims=True)
    pair_i = jnp.where(cfirsts < float(2 * SUBLANES), 0.0,
                       jnp.where(cfirsts < float(base), jnp.floor(cfirsts * (1.0 / SUBLANES)) - 1.0,
                                 cfirsts - float(base - SUBLANES)))
    lims = [jnp.sum(jnp.where(pair_i == float(i), 1.0, 0.0), axis=0, keepdims=True)
            for i in range(k)]
    e2 = jnp.exp(s_ref[1] - b[0:1])
    w = jnp.exp(s_ref[0] - a[0:1]) * (0.5 / z)
    row = _row_ids(s_ref.shape[1])
    for c in range(nchunk):
        sl = slice(c * LANES, (c + 1) * LANES)
        sl2 = slice(tl + c * LANES, tl + (c + 1) * LANES)
        r2 = jnp.full(row.shape, float(k), jnp.float32)
        ell = jnp.zeros(row.shape, jnp.float32)
        for r in range(k):
            r2 = jnp.where(row == firsts[r:r + 1, sl2], float(r), r2)
            ell = jnp.where(row == firsts[r:r + 1, sl], lims[r][:, sl], ell)
        r2_ref[0, c] = r2
        e2_ref[0, c] = e2[:, sl]
        ell_ref[0, c] = ell
        w_ref[0, c] = w[:, sl]


def peer_topk(scores, n_heads, *, tl=512):
    hs, nk, t = scores.shape
    assert hs == 2 * n_heads and PEER_TOPK == 2 * SUBLANES
    tl = min(tl, t)
    assert t % tl == 0 and tl % LANES == 0
    nchunk = tl // LANES
    ncand = (SUBLANES + 2) * SUBLANES
    shape = (n_heads, t // LANES, nk, LANES)
    ospec = pl.BlockSpec((1, nchunk, nk, LANES), lambda h, i: (h, i, 0, 0))
    return pl.pallas_call(
        _topk_kernel,
        grid=(n_heads, t // tl),
        in_specs=[pl.BlockSpec((2, nk, tl), lambda h, i: (h, 0, i))],
        out_specs=[ospec] * 4,
        out_shape=[jax.ShapeDtypeStruct(shape, jnp.float32)] * 4,
        scratch_shapes=[pltpu.VMEM((nk, 2 * tl), jnp.float32), pltpu.VMEM((ncand, tl), jnp.float32)],
        compiler_params=_cparams("parallel", "parallel"),
        name="peer_topk",
    )(scores)


def _peer_hidden_kernel(u_ref, xn_ref, o_ref, ub_ref):
    @pl.when(pl.program_id(1) == 0)
    def _():
        ub_ref[...] = u_ref[...].astype(jnp.bfloat16)

    hid = lax.dot_general(ub_ref[...], xn_ref[...], NT_DIMS, preferred_element_type=jnp.float32)
    o_ref[...] = hid * (1.0 + lax.erf(hid * INV_SQRT2))


def peer_hidden(xn, u, *, tb=1024, eb=1024):
    t, d = xn.shape
    e = u.shape[0]
    tb = min(tb, t)
    assert t % tb == 0 and e % eb == 0
    return pl.pallas_call(
        _peer_hidden_kernel,
        grid=(e // eb, t // tb),
        in_specs=[
            pl.BlockSpec((eb, d), lambda j, i: (j, 0)),
            pl.BlockSpec((tb, d), lambda j, i: (i, 0)),
        ],
        out_specs=pl.BlockSpec((eb, tb), lambda j, i: (j, i)),
        out_shape=jax.ShapeDtypeStruct((e, t), jnp.float32),
        scratch_shapes=[pltpu.VMEM((eb, d), jnp.bfloat16)],
        compiler_params=_cparams("parallel", "arbitrary"),
        name="peer_hidden",
    )(u, xn)


GATE_GROUP = 4


def _peer_dense_kernel(h_ref, vt_ref, r2_ref, e2_ref, ell_ref, w_ref, x_ref, gain_ref, o_ref,
                       acc_ref, a_ref, *, final_norm):
    n_heads, nchunk, nk, _ = r2_ref.shape
    na = ell_ref.shape[2]
    j = pl.program_id(1)

    @pl.when(j == 0)
    def _():
        acc_ref[...] = jnp.zeros_like(acc_ref)

    hb = nk // 2
    bf = jnp.bfloat16
    for a0 in range(0, na, GATE_GROUP):
        group = range(a0, a0 + GATE_GROUP)
        for c in range(nchunk):
            lanes = slice(c * LANES, (c + 1) * LANES)
            for half in range(2):
                keys = slice(half * hb, (half + 1) * hb)
                g = {ai: None for ai in group}
                for h in range(n_heads):
                    r2 = r2_ref[h, c, keys, :].astype(bf)
                    e2 = e2_ref[h, c, keys, :].astype(bf)
                    for ai in group:
                        lim = jnp.broadcast_to(ell_ref[h, c, ai:ai + 1, :], (hb, LANES)).astype(bf)
                        wa = jnp.broadcast_to(w_ref[h, c, ai:ai + 1, :], (hb, LANES)).astype(bf)
                        term = jnp.where(r2 < lim, e2, jnp.zeros_like(e2)) * wa
                        g[ai] = term if g[ai] is None else g[ai] + term
                for ai in group:
                    rows = slice(ai * nk + half * hb, ai * nk + (half + 1) * hb)
                    act = h_ref[rows, lanes]
                    a_ref[rows, lanes] = act.astype(bf) * g[ai]
    acc_ref[...] += jnp.dot(vt_ref[...], a_ref[...], preferred_element_type=jnp.float32)

    @pl.when(j == pl.num_programs(1) - 1)
    def _():
        x = x_ref[...] + acc_ref[...].T
        if final_norm:
            inv = lax.rsqrt(jnp.mean(x * x, axis=-1, keepdims=True) + RMS_EPS)
            x = x * inv * gain_ref[...]
        o_ref[...] = x


def peer_dense(act_t, vt, r2, e2, ell, w, x, gain, *, final_norm, tb=512, na=8):
    e, t = act_t.shape
    d = vt.shape[0]
    n_heads, _, nk, _ = r2.shape
    eb = na * nk
    tb = min(tb, t)
    assert t % tb == 0 and tb % LANES == 0 and e % eb == 0 and e == nk * nk and na % SUBLANES == 0
    nchunk = tb // LANES
    key_spec = pl.BlockSpec((n_heads, nchunk, nk, LANES), lambda i, j: (0, i, 0, 0))
    row_spec = pl.BlockSpec((n_heads, nchunk, na, LANES), lambda i, j: (0, i, j, 0))
    return pl.pallas_call(
        functools.partial(_peer_dense_kernel, final_norm=final_norm),
        grid=(t // tb, e // eb),
        in_specs=[
            pl.BlockSpec((eb, tb), lambda i, j: (j, i)),
            pl.BlockSpec((d, eb), lambda i, j: (0, j)),
            key_spec, key_spec, row_spec, row_spec,
            pl.BlockSpec((tb, d), lambda i, j: (i, 0)),
            pl.BlockSpec((1, d), lambda i, j: (0, 0)),
        ],
        out_specs=pl.BlockSpec((tb, d), lambda i, j: (i, 0)),
        out_shape=jax.ShapeDtypeStruct((t, d), jnp.float32),
        scratch_shapes=[pltpu.VMEM((d, tb), jnp.float32), pltpu.VMEM((eb, tb), jnp.bfloat16)],
        compiler_params=_cparams("parallel", "arbitrary"),
        name="peer_dense",
    )(act_t, vt, r2, e2, ell, w, x, gain.reshape(1, d))


ATT_BLK = 512


def _layer(x, norm_mix, w_in, conv_w, w_conv_out, w_attn_out, w_out, norm_ffn,
           peer_wq, peer_subkeys, peer_u, peer_v, norm_final, is_last):
    t, d = x.shape
    c = conv_w.shape[1]
    a = w_attn_out.shape[0]
    n_keys, half = peer_subkeys.shape[1:]
    peer_heads = peer_wq.shape[1] // (2 * half)
    dh = LANES // 2
    attn_heads = a // dh
    bf = jnp.bfloat16

    xn = rmsnorm(x, norm_mix, bf)
    pc = matmul(xn, w_in, 0, 3 * c, jnp.float32)
    q = matmul(xn, w_in, 3 * c, a, bf, scale=dh ** -0.5)
    k = matmul(xn, w_in, 3 * c + a, a, bf)
    vt = matmul_nt(w_in[:, 3 * c + 2 * a:3 * c + 3 * a].T.astype(bf), xn, bf, tm=min(ATT_BLK, t))
    gates = matmul(xn, w_in, 3 * c + 3 * a, 2 * d, jnp.float32)

    yc = conv_mixer(pc, conv_w)
    ya = attention(q, k, vt, attn_heads, dh, blk=min(ATT_BLK, t))
    merged = merge(yc, ya, w_conv_out.astype(bf), w_attn_out.astype(bf), gates)
    x1 = out_proj(merged, w_out.astype(bf), x)

    xn2 = rmsnorm(x1, norm_ffn, bf)
    scores = peer_scores(xn2, peer_wq, peer_subkeys.astype(bf), peer_heads)
    r2, e2, ell, w = peer_topk(scores, peer_heads)
    act_t = peer_hidden(xn2, peer_u)
    return peer_dense(act_t, peer_v.T.astype(bf), r2, e2, ell, w, x1, norm_final, final_norm=is_last)


def kernel(x, norm_mix, w_in, conv_w, w_conv_out, w_attn_out, w_out, norm_ffn, peer_wq,
           peer_subkeys, peer_u, peer_v, norm_final):
    b, t, d = x.shape
    depth = norm_mix.shape[0]
    outs = []
    for bi in range(b):
        xb = x[bi]
        for layer in range(depth):
            xb = _layer(xb, norm_mix[layer], w_in[layer], conv_w[layer], w_conv_out[layer],
                        w_attn_out[layer], w_out[layer], norm_ffn[layer], peer_wq[layer],
                        peer_subkeys[layer], peer_u[layer], peer_v[layer], norm_final,
                        is_last=layer == depth - 1)
        outs.append(xb)
    return jnp.stack(outs, axis=0)
```

```python
import functools

import jax
import jax.numpy as jnp
from jax import lax
from jax.experimental import pallas as pl
from jax.experimental.pallas import tpu as pltpu

RMS_EPS = 1e-6
LANES = 128
SUBLANES = 8
BF16_ROWS = 16
VMEM_LIMIT = 56 * 1024 * 1024
PEER_TOPK = 16
NEG_INF = float("-inf")
INV_SQRT2 = 0.7071067811865476
LOG2E = 1.4426950408889634
NT_DIMS = (((1,), (1,)), ((), ()))


def _cparams(*sem):
    return pltpu.CompilerParams(dimension_semantics=sem, vmem_limit_bytes=VMEM_LIMIT)


def _rmsnorm_kernel(x_ref, g_ref, o_ref):
    x = x_ref[...]
    inv = lax.rsqrt(jnp.mean(x * x, axis=-1, keepdims=True) + RMS_EPS)
    o_ref[...] = (x * inv * g_ref[...]).astype(o_ref.dtype)


def rmsnorm(x, gain, out_dtype, *, tm=512):
    t, d = x.shape
    tm = min(tm, t)
    assert t % tm == 0
    return pl.pallas_call(
        _rmsnorm_kernel,
        grid=(t // tm,),
        in_specs=[pl.BlockSpec((tm, d), lambda i: (i, 0)), pl.BlockSpec((1, d), lambda i: (0, 0))],
        out_specs=pl.BlockSpec((tm, d), lambda i: (i, 0)),
        out_shape=jax.ShapeDtypeStruct((t, d), out_dtype),
        compiler_params=_cparams("parallel"),
        name="rmsnorm",
    )(x, gain.reshape(1, d))


def _matmul_kernel(x_ref, w_ref, o_ref, wb_ref, *, scale):
    @pl.when(pl.program_id(1) == 0)
    def _():
        wb_ref[...] = w_ref[...].astype(jnp.bfloat16)

    acc = jnp.dot(x_ref[...], wb_ref[...], preferred_element_type=jnp.float32)
    if scale != 1.0:
        acc = acc * scale
    o_ref[...] = acc.astype(o_ref.dtype)


def matmul(x, w, col0, ncols, out_dtype, *, scale=1.0, tm=1024, tn=1024):
    t, k = x.shape
    tm = min(tm, t)
    tn = min(tn, ncols)
    assert t % tm == 0 and ncols % tn == 0 and col0 % tn == 0
    jb0 = col0 // tn
    return pl.pallas_call(
        functools.partial(_matmul_kernel, scale=scale),
        grid=(ncols // tn, t // tm),
        in_specs=[
            pl.BlockSpec((tm, k), lambda j, i: (i, 0)),
            pl.BlockSpec((k, tn), lambda j, i: (0, jb0 + j)),
        ],
        out_specs=pl.BlockSpec((tm, tn), lambda j, i: (i, j)),
        out_shape=jax.ShapeDtypeStruct((t, ncols), out_dtype),
        scratch_shapes=[pltpu.VMEM((k, tn), jnp.bfloat16)],
        compiler_params=_cparams("parallel", "arbitrary"),
        name="matmul",
    )(x, w)


def _matmul_nt_kernel(wt_ref, x_ref, o_ref):
    o_ref[0] = lax.dot_general(wt_ref[...], x_ref[...], NT_DIMS,
                               preferred_element_type=jnp.float32).astype(o_ref.dtype)


def matmul_nt(wt, x, out_dtype, *, tm, tn=1024):
    t, k = x.shape
    n = wt.shape[0]
    tn = min(tn, n)
    assert t % tm == 0 and n % tn == 0
    return pl.pallas_call(
        _matmul_nt_kernel,
        grid=(t // tm, n // tn),
        in_specs=[
            pl.BlockSpec((tn, k), lambda i, j: (j, 0)),
            pl.BlockSpec((tm, k), lambda i, j: (i, 0)),
        ],
        out_specs=pl.BlockSpec((1, tn, tm), lambda i, j: (i, j, 0)),
        out_shape=jax.ShapeDtypeStruct((t // tm, n, tm), out_dtype),
        compiler_params=_cparams("parallel", "parallel"),
        name="matmul_nt",
    )(wt, x)


def _conv_kernel(cb_ref, cc_ref, ch_ref, cch_ref, chh_ref, w_ref, o_ref, z_ref):
    tm = cb_ref.shape[0]
    z_halo = cch_ref[...] * chh_ref[...]
    z_ref[0:SUBLANES, :] = jnp.where(pl.program_id(0) == 0, 0.0, z_halo)
    z = cc_ref[...] * ch_ref[...]
    z_ref[SUBLANES:, :] = z
    w = w_ref[...]
    k = w.shape[0]
    y = z * w[k - 1:k, :]
    for j in range(1, k):
        y = y + z_ref[pl.ds(SUBLANES - j, tm), :] * w[k - 1 - j:k - j, :]
    o_ref[...] = (cb_ref[...] * y).astype(o_ref.dtype)


def conv_mixer(pc, conv_w, *, tm=512):
    t, c3 = pc.shape
    c = c3 // 3
    k = conv_w.shape[0]
    assert k - 1 <= SUBLANES
    tm = min(tm, t)
    assert t % tm == 0 and tm % SUBLANES == 0
    hb = tm // SUBLANES

    def halo_map(col):
        return lambda i: (jnp.maximum(i * hb - 1, 0), col)

    return pl.pallas_call(
        _conv_kernel,
        grid=(t // tm,),
        in_specs=[
            pl.BlockSpec((tm, c), lambda i: (i, 0)),
            pl.BlockSpec((tm, c), lambda i: (i, 1)),
            pl.BlockSpec((tm, c), lambda i: (i, 2)),
            pl.BlockSpec((SUBLANES, c), halo_map(1)),
            pl.BlockSpec((SUBLANES, c), halo_map(2)),
            pl.BlockSpec((k, c), lambda i: (0, 0)),
        ],
        out_specs=pl.BlockSpec((tm, c), lambda i: (i, 0)),
        out_shape=jax.ShapeDtypeStruct((t, c), jnp.bfloat16),
        scratch_shapes=[pltpu.VMEM((tm + SUBLANES, c), jnp.float32)],
        compiler_params=_cparams("parallel"),
        name="conv_mixer",
    )(pc, pc, pc, pc, pc, conv_w)


ATT_SUB = 128


def _attn_kernel(q_ref, k_ref, vt_ref, tri_ref, o_ref, acc_ref, *, dh):
    blk = q_ref.shape[0]
    nheads = 2 * (q_ref.shape[1] // LANES)
    nsub = blk // ATT_SUB
    i = pl.program_id(1)
    lane = lax.broadcasted_iota(jnp.int32, (1, LANES), 1)
    head0 = lane < dh
    qh = []
    for g in range(nheads // 2):
        q2 = q_ref[:, g * LANES:(g + 1) * LANES]
        zero = jnp.zeros_like(q2)
        qh += [jnp.where(head0, q2, zero), jnp.where(head0, zero, q2)]
    tri = tri_ref[...]
    acc_ref[...] = jnp.zeros_like(acc_ref)

    def block(jb, cars, diagonal):
        kb = k_ref[pl.ds(pl.multiple_of(jb * blk, blk), blk), :]
        vt = vt_ref[jb]
        new_cars = []
        zts = [lax.dot_general(kb[:, (h // 2) * LANES:(h // 2 + 1) * LANES], qh[h], NT_DIMS,
                               preferred_element_type=jnp.float32)
               for h in range(nheads)]
        for h in range(nheads):
            car = cars[h]
            ws = [None] * nsub
            for s in range(nsub - 1, -1, -1):
                z = zts[h][s * ATT_SUB:(s + 1) * ATT_SUB]
                sp = jnp.maximum(z, 0.0) + jnp.log(1.0 + jnp.exp2(jnp.abs(z) * -LOG2E))
                logsig = z - sp
                if diagonal:
                    key = lax.broadcasted_iota(jnp.int32, z.shape, 0) + s * ATT_SUB
                    qry = lax.broadcasted_iota(jnp.int32, z.shape, 1)
                    mask = key < qry
                    sp = jnp.where(mask, sp, 0.0)
                r = jnp.dot(tri, sp.astype(jnp.bfloat16), preferred_element_type=jnp.float32)
                sfx = r[:ATT_SUB]
                tot = r[ATT_SUB:ATT_SUB + 1]
                w = jnp.exp(logsig - sfx - car)
                if diagonal:
                    w = jnp.where(mask, w, 0.0)
                ws[s] = w.astype(jnp.bfloat16)
                car = car + tot
            wt = jnp.concatenate(ws, axis=0)
            acc_ref[h * dh:(h + 1) * dh, :] += jnp.dot(vt[h * dh:(h + 1) * dh, :], wt,
                                                       preferred_element_type=jnp.float32)
            new_cars.append(car)
        return tuple(new_cars)

    car0 = jnp.zeros((1, blk), jnp.float32)
    cars = block(i, (car0,) * nheads, True)

    def body(n, cars):
        return block(i - 1 - n, cars, False)

    lax.fori_loop(0, i, body, cars)
    o_ref[...] = acc_ref[...].T.astype(o_ref.dtype)


def _tri_matrix():
    s = jnp.arange(ATT_SUB)[:, None]
    j = jnp.arange(ATT_SUB)[None, :]
    later = (j > s).astype(jnp.bfloat16)
    ones = jnp.ones((BF16_ROWS, ATT_SUB), jnp.bfloat16)
    return jnp.concatenate([later, ones], axis=0)


def attention(q, k, vt, n_heads, dh, *, blk=512, heads_per_step=8):
    t = q.shape[0]
    heads_per_step = min(heads_per_step, n_heads)
    assert 2 * dh == LANES and heads_per_step % 2 == 0 and dh % BF16_ROWS == 0
    assert n_heads % heads_per_step == 0
    assert t % blk == 0 and blk % ATT_SUB == 0 and vt.shape == (t // blk, n_heads * dh, blk)
    ngroup = n_heads // heads_per_step
    gw = heads_per_step * dh
    nb = t // blk
    return pl.pallas_call(
        functools.partial(_attn_kernel, dh=dh),
        grid=(ngroup, nb),
        in_specs=[
            pl.BlockSpec((blk, gw), lambda p, i: (i, p)),
            pl.BlockSpec((t, gw), lambda p, i: (0, p), pipeline_mode=pl.Buffered(1)),
            pl.BlockSpec((nb, gw, blk), lambda p, i: (0, p, 0), pipeline_mode=pl.Buffered(1)),
            pl.BlockSpec((ATT_SUB + BF16_ROWS, ATT_SUB), lambda p, i: (0, 0)),
        ],
        out_specs=pl.BlockSpec((blk, gw), lambda p, i: (i, p)),
        out_shape=jax.ShapeDtypeStruct((t, n_heads * dh), jnp.bfloat16),
        scratch_shapes=[pltpu.VMEM((gw, blk), jnp.float32)],
        compiler_params=_cparams("parallel", "parallel"),
        name="stickbreak_attention",
    )(q, k, vt, _tri_matrix())


def _merge_kernel(yc_ref, ya_ref, wc_ref, wa_ref, gc_ref, ga_ref, o_ref):
    yc = jnp.dot(yc_ref[...], wc_ref[...], preferred_element_type=jnp.float32)
    ya = jnp.dot(ya_ref[...], wa_ref[...], preferred_element_type=jnp.float32)
    o_ref[...] = (jax.nn.sigmoid(gc_ref[...]) * yc + jax.nn.sigmoid(ga_ref[...]) * ya).astype(o_ref.dtype)


def merge(yc, ya, wc, wa, gates, *, tm=1024, tn=1024):
    t, c = yc.shape
    a = ya.shape[1]
    d = wc.shape[1]
    tm = min(tm, t)
    tn = min(tn, d)
    assert t % tm == 0 and d % tn == 0
    nj = d // tn
    return pl.pallas_call(
        _merge_kernel,
        grid=(t // tm, nj),
        in_specs=[
            pl.BlockSpec((tm, c), lambda i, j: (i, 0)),
            pl.BlockSpec((tm, a), lambda i, j: (i, 0)),
            pl.BlockSpec((c, tn), lambda i, j: (0, j)),
            pl.BlockSpec((a, tn), lambda i, j: (0, j)),
            pl.BlockSpec((tm, tn), lambda i, j: (i, j)),
            pl.BlockSpec((tm, tn), lambda i, j: (i, nj + j)),
        ],
        out_specs=pl.BlockSpec((tm, tn), lambda i, j: (i, j)),
        out_shape=jax.ShapeDtypeStruct((t, d), jnp.bfloat16),
        compiler_params=_cparams("parallel", "parallel"),
        name="merge_mixers",
    )(yc, ya, wc, wa, gates, gates)


def _out_proj_kernel(m_ref, w_ref, x_ref, o_ref):
    o_ref[...] = x_ref[...] + jnp.dot(m_ref[...], w_ref[...], preferred_element_type=jnp.float32)


def out_proj(merged, w, x, *, tm=1024, tn=1024):
    t, d = merged.shape
    n = w.shape[1]
    tm = min(tm, t)
    tn = min(tn, n)
    assert t % tm == 0 and n % tn == 0
    return pl.pallas_call(
        _out_proj_kernel,
        grid=(t // tm, n // tn),
        in_specs=[
            pl.BlockSpec((tm, d), lambda i, j: (i, 0)),
            pl.BlockSpec((d, tn), lambda i, j: (0, j)),
            pl.BlockSpec((tm, tn), lambda i, j: (i, j)),
        ],
        out_specs=pl.BlockSpec((tm, tn), lambda i, j: (i, j)),
        out_shape=jax.ShapeDtypeStruct((t, n), jnp.float32),
        compiler_params=_cparams("parallel", "parallel"),
        name="out_proj",
    )(merged, w, x)


def _scores_kernel(x_ref, w_ref, k_ref, o_ref, wb_ref):
    @pl.when(pl.program_id(1) == 0)
    def _():
        wb_ref[...] = w_ref[...].astype(jnp.bfloat16)

    half = k_ref.shape[2]
    q = jnp.dot(x_ref[...], wb_ref[...], preferred_element_type=jnp.float32)
    for g in range(o_ref.shape[0]):
        qg = q[:, g * half:(g + 1) * half].astype(jnp.bfloat16)
        keys = k_ref[g % 2]
        o_ref[g] = lax.dot_general(keys, qg, NT_DIMS, preferred_element_type=jnp.float32)


def peer_scores(xn, wq, subkeys, n_heads, *, tm=1024, tn=1024):
    t, d = xn.shape
    _, nk, half = subkeys.shape
    qd = wq.shape[1]
    assert qd == n_heads * 2 * half
    tm = min(tm, t)
    tn = min(tn, qd)
    assert t % tm == 0 and qd % tn == 0 and tn % (2 * half) == 0
    gpt = tn // half
    return pl.pallas_call(
        _scores_kernel,
        grid=(qd // tn, t // tm),
        in_specs=[
            pl.BlockSpec((tm, d), lambda j, i: (i, 0)),
            pl.BlockSpec((d, tn), lambda j, i: (0, j)),
            pl.BlockSpec((2, nk, half), lambda j, i: (0, 0, 0)),
        ],
        out_specs=pl.BlockSpec((gpt, nk, tm), lambda j, i: (j, 0, i)),
        out_shape=jax.ShapeDtypeStruct((2 * n_heads, nk, t), jnp.float32),
        scratch_shapes=[pltpu.VMEM((d, tn), jnp.bfloat16)],
        compiler_params=_cparams("parallel", "arbitrary"),
        name="peer_scores",
    )(xn, wq, subkeys)


def _row_ids(rows):
    return lax.broadcasted_iota(jnp.int32, (rows, LANES), 0).astype(jnp.float32)


def _extract_top(cur_ref, count):
    rows, width = cur_ref.shape
    ntile = width // LANES
    row = _row_ids(rows)
    slot = _row_ids(count)

    def body(r, carry):
        vals, firsts = carry
        rf = lax.convert_element_type(r, jnp.float32)
        new_vals, new_firsts = [], []
        for tile in range(ntile):
            sl = slice(tile * LANES, (tile + 1) * LANES)
            cur = cur_ref[:, sl]
            m = jnp.max(cur, axis=0, keepdims=True)
            first = jnp.min(jnp.where(cur == m, row, float(rows)), axis=0, keepdims=True)
            cur_ref[:, sl] = jnp.where(row == first, NEG_INF, cur)
            new_vals.append(jnp.where(slot == rf, m, vals[:, sl]))
            new_firsts.append(jnp.where(slot == rf, first, firsts[:, sl]))
        return jnp.concatenate(new_vals, axis=1), jnp.concatenate(new_firsts, axis=1)

    init = jnp.zeros((count, width), jnp.float32)
    return lax.fori_loop(0, count, body, (init, init))


def _extract_distinct(cur_ref, count):
    rows, width = cur_ref.shape
    slot = _row_ids(count)

    def body(r, vals):
        rf = lax.convert_element_type(r, jnp.float32)
        new_vals = []
        for tile in range(width // LANES):
            sl = slice(tile * LANES, (tile + 1) * LANES)
            cur = cur_ref[:, sl]
            m = jnp.max(cur, axis=0, keepdims=True)
            cur_ref[:, sl] = jnp.where(cur == m, NEG_INF, cur)
            new_vals.append(jnp.where(slot == rf, m, vals[:, sl]))
        return jnp.concatenate(new_vals, axis=1)

    return lax.fori_loop(0, count, body, jnp.zeros((count, width), jnp.float32))


def _removed(cur_ref):
    return jnp.sum(jnp.where(cur_ref[...] == NEG_INF, 1.0, 0.0), axis=0, keepdims=True)


PAIR_BASE = (SUBLANES + 1) * SUBLANES
PAIR_ROWS = PAIR_BASE + SUBLANES
PAIR_COUNT = sum(PEER_TOPK // (i + 1) for i in range(PEER_TOPK))


def _fill_pairs(cand_ref, a, b):
    k = PEER_TOPK
    sub = lax.broadcasted_iota(jnp.int32, (SUBLANES, a.shape[1]), 0)
    cand_ref[0:SUBLANES] = a[0:1] + b[0:SUBLANES]
    cand_ref[SUBLANES:2 * SUBLANES] = a[0:1] + b[SUBLANES:2 * SUBLANES]
    for i in range(1, SUBLANES):
        cand_ref[(i + 1) * SUBLANES:(i + 2) * SUBLANES] = jnp.where(
            sub < k // (i + 1), a[i:i + 1] + b[0:SUBLANES], NEG_INF)
    cand_ref[PAIR_BASE:PAIR_ROWS] = a[SUBLANES:] + b[0:1]


def _topk_kernel(s_ref, r2_ref, e2_ref, ell_ref, w_ref, cur_ref, cand_ref):
    k = PEER_TOPK
    nchunk = r2_ref.shape[1]
    nk = s_ref.shape[1]
    tl = nchunk * LANES
    cur_ref[:, :tl] = s_ref[0]
    cur_ref[:, tl:] = s_ref[1]
    vals = _extract_distinct(cur_ref, k)
    gone = _removed(cur_ref)
    a = vals[:, :tl]
    b = vals[:, tl:]
    _fill_pairs(cand_ref, a, b)
    cvals = _extract_distinct(cand_ref, k)
    expect_pairs = float(PAIR_ROWS - PAIR_COUNT + k)
    ties = jnp.max(jnp.abs(gone[:, :tl] - float(k)) + jnp.abs(gone[:, tl:] - float(k))
                   + jnp.abs(_removed(cand_ref) - expect_pairs))

    def emit(a, b, cvals, lims, r2_of, ell_of):
        z = jnp.sum(jnp.exp(cvals - cvals[0:1]), axis=0, keepdims=True)
        e2 = jnp.exp(s_ref[1] - b[0:1])
        w = jnp.exp(s_ref[0] - a[0:1]) * (0.5 / z)
        for c in range(nchunk):
            sl = slice(c * LANES, (c + 1) * LANES)
            r2 = jnp.full((nk, LANES), float(k), jnp.float32)
            ell = jnp.zeros((nk, LANES), jnp.float32)
            for r in range(k - 1, -1, -1):
                r2 = jnp.where(r2_of(r, c), float(r), r2)
                ell = jnp.where(ell_of(r, c), lims[r][:, sl], ell)
            r2_ref[0, c] = r2
            e2_ref[0, c] = e2[:, sl]
            ell_ref[0, c] = ell
            w_ref[0, c] = w[:, sl]

    @pl.when(ties == 0.0)
    def _():
        took = jnp.where(cand_ref[...] == NEG_INF, 1.0, 0.0)
        lims = [jnp.sum(took[0:2 * SUBLANES], axis=0, keepdims=True)]
        for i in range(1, SUBLANES):
            pad = float(SUBLANES - k // (i + 1))
            lims.append(jnp.sum(took[(i + 1) * SUBLANES:(i + 2) * SUBLANES], axis=0, keepdims=True) - pad)
        lims += [took[PAIR_BASE + j:PAIR_BASE + j + 1] for j in range(SUBLANES)]

        def r2_of(r, c):
            return s_ref[1, :, c * LANES:(c + 1) * LANES] == b[r:r + 1, c * LANES:(c + 1) * LANES]

        def ell_of(r, c):
            return s_ref[0, :, c * LANES:(c + 1) * LANES] == a[r:r + 1, c * LANES:(c + 1) * LANES]

        emit(a, b, cvals, lims, r2_of, ell_of)

    @pl.when(ties != 0.0)
    def _():
        cur_ref[:, :tl] = s_ref[0]
        cur_ref[:, tl:] = s_ref[1]
        vals, firsts = _extract_top(cur_ref, k)
        a = vals[:, :tl]
        b = vals[:, tl:]
        _fill_pairs(cand_ref, a, b)
        cvals, cfirsts = _extract_top(cand_ref, k)
        pair_i = jnp.where(cfirsts < float(2 * SUBLANES), 0.0,
                           jnp.where(cfirsts < float(PAIR_BASE),
                                     jnp.floor(cfirsts * (1.0 / SUBLANES)) - 1.0,
                                     cfirsts - float(PAIR_BASE - SUBLANES)))
        lims = [jnp.sum(jnp.where(pair_i == float(i), 1.0, 0.0), axis=0, keepdims=True)
                for i in range(k)]
        row = _row_ids(nk)

        def r2_of(r, c):
            return row == firsts[r:r + 1, tl + c * LANES:tl + (c + 1) * LANES]

        def ell_of(r, c):
            return row == firsts[r:r + 1, c * LANES:(c + 1) * LANES]

        emit(a, b, cvals, lims, r2_of, ell_of)


def peer_topk(scores, n_heads, *, tl=512):
    hs, nk, t = scores.shape
    assert hs == 2 * n_heads and PEER_TOPK == 2 * SUBLANES
    tl = min(tl, t)
    assert t % tl == 0 and tl % LANES == 0
    nchunk = tl // LANES
    ncand = (SUBLANES + 2) * SUBLANES
    shape = (n_heads, t // LANES, nk, LANES)
    ospec = pl.BlockSpec((1, nchunk, nk, LANES), lambda h, i: (h, i, 0, 0))
    return pl.pallas_call(
        _topk_kernel,
        grid=(n_heads, t // tl),
        in_specs=[pl.BlockSpec((2, nk, tl), lambda h, i: (h, 0, i))],
        out_specs=[ospec] * 4,
        out_shape=[jax.ShapeDtypeStruct(shape, jnp.float32)] * 4,
        scratch_shapes=[pltpu.VMEM((nk, 2 * tl), jnp.float32), pltpu.VMEM((ncand, tl), jnp.float32)],
        compiler_params=_cparams("parallel", "parallel"),
        name="peer_topk",
    )(scores)


def _peer_hidden_kernel(u_ref, xn_ref, o_ref, ub_ref):
    @pl.when(pl.program_id(1) == 0)
    def _():
        ub_ref[...] = u_ref[...].astype(jnp.bfloat16)

    hid = lax.dot_general(ub_ref[...], xn_ref[...], NT_DIMS, preferred_element_type=jnp.float32)
    o_ref[...] = hid * (1.0 + lax.erf(hid * INV_SQRT2))


def peer_hidden(xn, u, *, tb=1024, eb=1024):
    t, d = xn.shape
    e = u.shape[0]
    tb = min(tb, t)
    assert t % tb == 0 and e % eb == 0
    return pl.pallas_call(
        _peer_hidden_kernel,
        grid=(e // eb, t // tb),
        in_specs=[
            pl.BlockSpec((eb, d), lambda j, i: (j, 0)),
            pl.BlockSpec((tb, d), lambda j, i: (i, 0)),
        ],
        out_specs=pl.BlockSpec((eb, tb), lambda j, i: (j, i)),
        out_shape=jax.ShapeDtypeStruct((e, t), jnp.float32),
        scratch_shapes=[pltpu.VMEM((eb, d), jnp.bfloat16)],
        compiler_params=_cparams("parallel", "arbitrary"),
        name="peer_hidden",
    )(u, xn)


GATE_GROUP = 4


def _peer_dense_kernel(h_ref, vt_ref, r2_ref, e2_ref, ell_ref, w_ref, x_ref, gain_ref, o_ref,
                       acc_ref, a_ref, *, final_norm):
    n_heads, nchunk, nk, _ = r2_ref.shape
    na = ell_ref.shape[2]
    j = pl.program_id(1)

    @pl.when(j == 0)
    def _():
        acc_ref[...] = jnp.zeros_like(acc_ref)

    hb = nk // 2
    bf = jnp.bfloat16
    for a0 in range(0, na, GATE_GROUP):
        group = range(a0, a0 + GATE_GROUP)
        for c in range(nchunk):
            lanes = slice(c * LANES, (c + 1) * LANES)
            for half in range(2):
                keys = slice(half * hb, (half + 1) * hb)
                g = {ai: None for ai in group}
                for h in range(n_heads):
                    r2 = r2_ref[h, c, keys, :].astype(bf)
                    e2 = e2_ref[h, c, keys, :].astype(bf)
                    for ai in group:
                        lim = jnp.broadcast_to(ell_ref[h, c, ai:ai + 1, :], (hb, LANES)).astype(bf)
                        wa = jnp.broadcast_to(w_ref[h, c, ai:ai + 1, :], (hb, LANES)).astype(bf)
                        term = jnp.where(r2 < lim, e2, jnp.zeros_like(e2)) * wa
                        g[ai] = term if g[ai] is None else g[ai] + term
                for ai in group:
                    rows = slice(ai * nk + half * hb, ai * nk + (half + 1) * hb)
                    act = h_ref[rows, lanes]
                    a_ref[rows, lanes] = act.astype(bf) * g[ai]
    acc_ref[...] += jnp.dot(vt_ref[...], a_ref[...], preferred_element_type=jnp.float32)

    @pl.when(j == pl.num_programs(1) - 1)
    def _():
        x = x_ref[...] + acc_ref[...].T
        if final_norm:
            inv = lax.rsqrt(jnp.mean(x * x, axis=-1, keepdims=True) + RMS_EPS)
            x = x * inv * gain_ref[...]
        o_ref[...] = x


def peer_dense(act_t, vt, r2, e2, ell, w, x, gain, *, final_norm, tb=512, na=8):
    e, t = act_t.shape
    d = vt.shape[0]
    n_heads, _, nk, _ = r2.shape
    eb = na * nk
    tb = min(tb, t)
    assert t % tb == 0 and tb % LANES == 0 and e % eb == 0 and e == nk * nk and na % SUBLANES == 0
    nchunk = tb // LANES
    key_spec = pl.BlockSpec((n_heads, nchunk, nk, LANES), lambda i, j: (0, i, 0, 0))
    row_spec = pl.BlockSpec((n_heads, nchunk, na, LANES), lambda i, j: (0, i, j, 0))
    return pl.pallas_call(
        functools.partial(_peer_dense_kernel, final_norm=final_norm),
        grid=(t // tb, e // eb),
        in_specs=[
            pl.BlockSpec((eb, tb), lambda i, j: (j, i)),
            pl.BlockSpec((d, eb), lambda i, j: (0, j)),
            key_spec, key_spec, row_spec, row_spec,
            pl.BlockSpec((tb, d), lambda i, j: (i, 0)),
            pl.BlockSpec((1, d), lambda i, j: (0, 0)),
        ],
        out_specs=pl.BlockSpec((tb, d), lambda i, j: (i, 0)),
        out_shape=jax.ShapeDtypeStruct((t, d), jnp.float32),
        scratch_shapes=[pltpu.VMEM((d, tb), jnp.float32), pltpu.VMEM((eb, tb), jnp.bfloat16)],
        compiler_params=_cparams("parallel", "arbitrary"),
        name="peer_dense",
    )(act_t, vt, r2, e2, ell, w, x, gain.reshape(1, d))


ATT_BLK = 512


def _layer(x, norm_mix, w_in, conv_w, w_conv_out, w_attn_out, w_out, norm_ffn,
           peer_wq, peer_subkeys, peer_u, peer_v, norm_final, is_last):
    t, d = x.shape
    c = conv_w.shape[1]
    a = w_attn_out.shape[0]
    n_keys, half = peer_subkeys.shape[1:]
    peer_heads = peer_wq.shape[1] // (2 * half)
    dh = LANES // 2
    attn_heads = a // dh
    bf = jnp.bfloat16

    xn = rmsnorm(x, norm_mix, bf)
    pc = matmul(xn, w_in, 0, 3 * c, jnp.float32)
    q = matmul(xn, w_in, 3 * c, a, bf, scale=dh ** -0.5)
    k = matmul(xn, w_in, 3 * c + a, a, bf)
    vt = matmul_nt(w_in[:, 3 * c + 2 * a:3 * c + 3 * a].T.astype(bf), xn, bf, tm=min(ATT_BLK, t))
    gates = matmul(xn, w_in, 3 * c + 3 * a, 2 * d, jnp.float32)

    yc = conv_mixer(pc, conv_w)
    ya = attention(q, k, vt, attn_heads, dh, blk=min(ATT_BLK, t))
    merged = merge(yc, ya, w_conv_out.astype(bf), w_attn_out.astype(bf), gates)
    x1 = out_proj(merged, w_out.astype(bf), x)

    xn2 = rmsnorm(x1, norm_ffn, bf)
    scores = peer_scores(xn2, peer_wq, peer_subkeys.astype(bf), peer_heads)
    r2, e2, ell, w = peer_topk(scores, peer_heads)
    act_t = peer_hidden(xn2, peer_u)
    return peer_dense(act_t, peer_v.T.astype(bf), r2, e2, ell, w, x1, norm_final, final_norm=is_last)


def kernel(x, norm_mix, w_in, conv_w, w_conv_out, w_attn_out, w_out, norm_ffn, peer_wq,
           peer_subkeys, peer_u, peer_v, norm_final):
    b, t, d = x.shape
    depth = norm_mix.shape[0]
    outs = []
    for bi in range(b):
        xb = x[bi]
        for layer in range(depth):
            xb = _layer(xb, norm_mix[layer], w_in[layer], conv_w[layer], w_conv_out[layer],
                        w_attn_out[layer], w_out[layer], norm_ffn[layer], peer_wq[layer],
                        peer_subkeys[layer], peer_u[layer], peer_v[layer], norm_final,
                        is_last=layer == depth - 1)
        outs.append(xb)
    return jnp.stack(outs, axis=0)
```

```python
import functools

import jax
import jax.numpy as jnp
from jax import lax
from jax.experimental import pallas as pl
from jax.experimental.pallas import tpu as pltpu

RMS_EPS = 1e-6
LANES = 128
SUBLANES = 8
BF16_ROWS = 16
VMEM_LIMIT = 56 * 1024 * 1024
PEER_TOPK = 16
NEG_INF = float("-inf")
INV_SQRT2 = 0.7071067811865476
LOG2E = 1.4426950408889634
NT_DIMS = (((1,), (1,)), ((), ()))


def _cparams(*sem):
    return pltpu.CompilerParams(dimension_semantics=sem, vmem_limit_bytes=VMEM_LIMIT)


def _rmsnorm_kernel(x_ref, g_ref, o_ref):
    x = x_ref[...]
    inv = lax.rsqrt(jnp.mean(x * x, axis=-1, keepdims=True) + RMS_EPS)
    o_ref[...] = (x * inv * g_ref[...]).astype(o_ref.dtype)


def rmsnorm(x, gain, out_dtype, *, tm=512):
    t, d = x.shape
    tm = min(tm, t)
    assert t % tm == 0
    return pl.pallas_call(
        _rmsnorm_kernel,
        grid=(t // tm,),
        in_specs=[pl.BlockSpec((tm, d), lambda i: (i, 0)), pl.BlockSpec((1, d), lambda i: (0, 0))],
        out_specs=pl.BlockSpec((tm, d), lambda i: (i, 0)),
        out_shape=jax.ShapeDtypeStruct((t, d), out_dtype),
        compiler_params=_cparams("parallel"),
        name="rmsnorm",
    )(x, gain.reshape(1, d))


def _matmul_kernel(x_ref, w_ref, o_ref, wb_ref, *, scale, sigmoid):
    @pl.when(pl.program_id(1) == 0)
    def _():
        wb_ref[...] = w_ref[...].astype(jnp.bfloat16)

    acc = jnp.dot(x_ref[...], wb_ref[...], preferred_element_type=jnp.float32)
    if scale != 1.0:
        acc = acc * scale
    if sigmoid:
        acc = jax.nn.sigmoid(acc)
    o_ref[...] = acc.astype(o_ref.dtype)


def matmul(x, w, col0, ncols, out_dtype, *, scale=1.0, sigmoid=False, tm=1024, tn=1024):
    t, k = x.shape
    tm = min(tm, t)
    tn = min(tn, ncols)
    assert t % tm == 0 and ncols % tn == 0 and col0 % tn == 0
    jb0 = col0 // tn
    return pl.pallas_call(
        functools.partial(_matmul_kernel, scale=scale, sigmoid=sigmoid),
        grid=(ncols // tn, t // tm),
        in_specs=[
            pl.BlockSpec((tm, k), lambda j, i: (i, 0)),
            pl.BlockSpec((k, tn), lambda j, i: (0, jb0 + j)),
        ],
        out_specs=pl.BlockSpec((tm, tn), lambda j, i: (i, j)),
        out_shape=jax.ShapeDtypeStruct((t, ncols), out_dtype),
        scratch_shapes=[pltpu.VMEM((k, tn), jnp.bfloat16)],
        compiler_params=_cparams("parallel", "arbitrary"),
        name="matmul",
    )(x, w)


def _matmul_nt_kernel(wt_ref, x_ref, o_ref):
    o_ref[0] = lax.dot_general(wt_ref[...], x_ref[...], NT_DIMS,
                               preferred_element_type=jnp.float32).astype(o_ref.dtype)


def matmul_nt(wt, x, out_dtype, *, tm, tn=1024):
    t, k = x.shape
    n = wt.shape[0]
    tn = min(tn, n)
    assert t % tm == 0 and n % tn == 0
    return pl.pallas_call(
        _matmul_nt_kernel,
        grid=(t // tm, n // tn),
        in_specs=[
            pl.BlockSpec((tn, k), lambda i, j: (j, 0)),
            pl.BlockSpec((tm, k), lambda i, j: (i, 0)),
        ],
        out_specs=pl.BlockSpec((1, tn, tm), lambda i, j: (i, j, 0)),
        out_shape=jax.ShapeDtypeStruct((t // tm, n, tm), out_dtype),
        compiler_params=_cparams("parallel", "parallel"),
        name="matmul_nt",
    )(wt, x)


def _conv_kernel(cb_ref, cc_ref, ch_ref, cch_ref, chh_ref, w_ref, o_ref, z_ref):
    tm = cb_ref.shape[0]
    z_halo = cch_ref[...] * chh_ref[...]
    z_ref[0:SUBLANES, :] = jnp.where(pl.program_id(0) == 0, 0.0, z_halo)
    z = cc_ref[...] * ch_ref[...]
    z_ref[SUBLANES:, :] = z
    w = w_ref[...]
    k = w.shape[0]
    y = z * w[k - 1:k, :]
    for j in range(1, k):
        y = y + z_ref[pl.ds(SUBLANES - j, tm), :] * w[k - 1 - j:k - j, :]
    o_ref[...] = (cb_ref[...] * y).astype(o_ref.dtype)


def conv_mixer(pc, conv_w, *, tm=512):
    t, c3 = pc.shape
    c = c3 // 3
    k = conv_w.shape[0]
    assert k - 1 <= SUBLANES
    tm = min(tm, t)
    assert t % tm == 0 and tm % SUBLANES == 0
    hb = tm // SUBLANES

    def halo_map(col):
        return lambda i: (jnp.maximum(i * hb - 1, 0), col)

    return pl.pallas_call(
        _conv_kernel,
        grid=(t // tm,),
        in_specs=[
            pl.BlockSpec((tm, c), lambda i: (i, 0)),
            pl.BlockSpec((tm, c), lambda i: (i, 1)),
            pl.BlockSpec((tm, c), lambda i: (i, 2)),
            pl.BlockSpec((SUBLANES, c), halo_map(1)),
            pl.BlockSpec((SUBLANES, c), halo_map(2)),
            pl.BlockSpec((k, c), lambda i: (0, 0)),
        ],
        out_specs=pl.BlockSpec((tm, c), lambda i: (i, 0)),
        out_shape=jax.ShapeDtypeStruct((t, c), jnp.bfloat16),
        scratch_shapes=[pltpu.VMEM((tm + SUBLANES, c), jnp.float32)],
        compiler_params=_cparams("parallel"),
        name="conv_mixer",
    )(pc, pc, pc, pc, pc, conv_w)


ATT_SUB = 128


def _attn_kernel(q_ref, k_ref, vt_ref, tri_ref, o_ref, acc_ref, *, dh):
    blk = q_ref.shape[0]
    nheads = 2 * (q_ref.shape[1] // LANES)
    nsub = blk // ATT_SUB
    i = pl.program_id(1)
    lane = lax.broadcasted_iota(jnp.int32, (1, LANES), 1)
    head0 = lane < dh
    qh = []
    for g in range(nheads // 2):
        q2 = q_ref[:, g * LANES:(g + 1) * LANES]
        zero = jnp.zeros_like(q2)
        qh += [jnp.where(head0, q2, zero), jnp.where(head0, zero, q2)]
    tri = tri_ref[...]
    acc_ref[...] = jnp.zeros_like(acc_ref)

    def block(jb, cars, diagonal):
        kb = k_ref[pl.ds(pl.multiple_of(jb * blk, blk), blk), :]
        vt = vt_ref[jb]
        new_cars = []
        zts = [lax.dot_general(kb[:, (h // 2) * LANES:(h // 2 + 1) * LANES], qh[h], NT_DIMS,
                               preferred_element_type=jnp.float32)
               for h in range(nheads)]
        for h in range(nheads):
            car = cars[h]
            ws = [None] * nsub
            for s in range(nsub - 1, -1, -1):
                z = zts[h][s * ATT_SUB:(s + 1) * ATT_SUB]
                sp = jnp.maximum(z, 0.0) + jnp.log(1.0 + jnp.exp2(jnp.abs(z) * -LOG2E))
                logsig = z - sp
                if diagonal:
                    key = lax.broadcasted_iota(jnp.int32, z.shape, 0) + s * ATT_SUB
                    qry = lax.broadcasted_iota(jnp.int32, z.shape, 1)
                    mask = key < qry
                    sp = jnp.where(mask, sp, 0.0)
                r = jnp.dot(tri, sp.astype(jnp.bfloat16), preferred_element_type=jnp.float32)
                sfx = r[:ATT_SUB]
                tot = r[ATT_SUB:ATT_SUB + 1]
                w = jnp.exp(logsig - sfx - car)
                if diagonal:
                    w = jnp.where(mask, w, 0.0)
                ws[s] = w.astype(jnp.bfloat16)
                car = car + tot
            wt = jnp.concatenate(ws, axis=0)
            acc_ref[h * dh:(h + 1) * dh, :] += jnp.dot(vt[h * dh:(h + 1) * dh, :], wt,
                                                       preferred_element_type=jnp.float32)
            new_cars.append(car)
        return tuple(new_cars)

    car0 = jnp.zeros((1, blk), jnp.float32)
    cars = block(i, (car0,) * nheads, True)

    def body(n, cars):
        return block(i - 1 - n, cars, False)

    lax.fori_loop(0, i, body, cars)
    o_ref[...] = acc_ref[...].T.astype(o_ref.dtype)


def _tri_matrix():
    s = jnp.arange(ATT_SUB)[:, None]
    j = jnp.arange(ATT_SUB)[None, :]
    later = (j > s).astype(jnp.bfloat16)
    ones = jnp.ones((BF16_ROWS, ATT_SUB), jnp.bfloat16)
    return jnp.concatenate([later, ones], axis=0)


def attention(q, k, vt, n_heads, dh, *, blk=512, heads_per_step=8):
    t = q.shape[0]
    heads_per_step = min(heads_per_step, n_heads)
    assert 2 * dh == LANES and heads_per_step % 2 == 0 and dh % BF16_ROWS == 0
    assert n_heads % heads_per_step == 0
    assert t % blk == 0 and blk % ATT_SUB == 0 and vt.shape == (t // blk, n_heads * dh, blk)
    ngroup = n_heads // heads_per_step
    gw = heads_per_step * dh
    nb = t // blk
    return pl.pallas_call(
        functools.partial(_attn_kernel, dh=dh),
        grid=(ngroup, nb),
        in_specs=[
            pl.BlockSpec((blk, gw), lambda p, i: (i, p)),
            pl.BlockSpec((t, gw), lambda p, i: (0, p), pipeline_mode=pl.Buffered(1)),
            pl.BlockSpec((nb, gw, blk), lambda p, i: (0, p, 0), pipeline_mode=pl.Buffered(1)),
            pl.BlockSpec((ATT_SUB + BF16_ROWS, ATT_SUB), lambda p, i: (0, 0)),
        ],
        out_specs=pl.BlockSpec((blk, gw), lambda p, i: (i, p)),
        out_shape=jax.ShapeDtypeStruct((t, n_heads * dh), jnp.bfloat16),
        scratch_shapes=[pltpu.VMEM((gw, blk), jnp.float32)],
        compiler_params=_cparams("parallel", "parallel"),
        name="stickbreak_attention",
    )(q, k, vt, _tri_matrix())


def _merge_kernel(yc_ref, ya_ref, wc_ref, wa_ref, gc_ref, ga_ref, o_ref):
    yc = jnp.dot(yc_ref[...], wc_ref[...], preferred_element_type=jnp.float32)
    ya = jnp.dot(ya_ref[...], wa_ref[...], preferred_element_type=jnp.float32)
    o_ref[...] = (gc_ref[...].astype(jnp.float32) * yc
                  + ga_ref[...].astype(jnp.float32) * ya).astype(o_ref.dtype)


def merge(yc, ya, wc, wa, gates, *, tm=1024, tn=1024):
    t, c = yc.shape
    a = ya.shape[1]
    d = wc.shape[1]
    tm = min(tm, t)
    tn = min(tn, d)
    assert t % tm == 0 and d % tn == 0
    nj = d // tn
    return pl.pallas_call(
        _merge_kernel,
        grid=(t // tm, nj),
        in_specs=[
            pl.BlockSpec((tm, c), lambda i, j: (i, 0)),
            pl.BlockSpec((tm, a), lambda i, j: (i, 0)),
            pl.BlockSpec((c, tn), lambda i, j: (0, j)),
            pl.BlockSpec((a, tn), lambda i, j: (0, j)),
            pl.BlockSpec((tm, tn), lambda i, j: (i, j)),
            pl.BlockSpec((tm, tn), lambda i, j: (i, nj + j)),
        ],
        out_specs=pl.BlockSpec((tm, tn), lambda i, j: (i, j)),
        out_shape=jax.ShapeDtypeStruct((t, d), jnp.bfloat16),
        compiler_params=_cparams("parallel", "parallel"),
        name="merge_mixers",
    )(yc, ya, wc, wa, gates, gates)


def _out_proj_kernel(m_ref, w_ref, x_ref, g_ref, o_ref, xn_ref):
    x1 = x_ref[...] + jnp.dot(m_ref[...], w_ref[...], preferred_element_type=jnp.float32)
    o_ref[...] = x1
    inv = lax.rsqrt(jnp.mean(x1 * x1, axis=-1, keepdims=True) + RMS_EPS)
    xn_ref[...] = (x1 * inv * g_ref[...]).astype(xn_ref.dtype)


def out_proj(merged, w, x, gain, *, tm=512):
    t, d = merged.shape
    n = w.shape[1]
    tm = min(tm, t)
    assert t % tm == 0 and x.shape == (t, n)
    return pl.pallas_call(
        _out_proj_kernel,
        grid=(t // tm,),
        in_specs=[
            pl.BlockSpec((tm, d), lambda i: (i, 0)),
            pl.BlockSpec((d, n), lambda i: (0, 0), pipeline_mode=pl.Buffered(1)),
            pl.BlockSpec((tm, n), lambda i: (i, 0)),
            pl.BlockSpec((1, n), lambda i: (0, 0)),
        ],
        out_specs=[pl.BlockSpec((tm, n), lambda i: (i, 0)), pl.BlockSpec((tm, n), lambda i: (i, 0))],
        out_shape=[jax.ShapeDtypeStruct((t, n), jnp.float32), jax.ShapeDtypeStruct((t, n), jnp.bfloat16)],
        compiler_params=_cparams("parallel"),
        name="out_proj",
    )(merged, w, x, gain.reshape(1, n))


def _scores_kernel(x_ref, w_ref, k_ref, o_ref, wb_ref):
    @pl.when(pl.program_id(1) == 0)
    def _():
        wb_ref[...] = w_ref[...].astype(jnp.bfloat16)

    half = k_ref.shape[2]
    q = jnp.dot(x_ref[...], wb_ref[...], preferred_element_type=jnp.float32)
    for g in range(o_ref.shape[0]):
        qg = q[:, g * half:(g + 1) * half].astype(jnp.bfloat16)
        keys = k_ref[g % 2]
        o_ref[g] = lax.dot_general(keys, qg, NT_DIMS, preferred_element_type=jnp.float32)


def peer_scores(xn, wq, subkeys, n_heads, *, tm=1024, tn=1024):
    t, d = xn.shape
    _, nk, half = subkeys.shape
    qd = wq.shape[1]
    assert qd == n_heads * 2 * half
    tm = min(tm, t)
    tn = min(tn, qd)
    assert t % tm == 0 and qd % tn == 0 and tn % (2 * half) == 0
    gpt = tn // half
    return pl.pallas_call(
        _scores_kernel,
        grid=(qd // tn, t // tm),
        in_specs=[
            pl.BlockSpec((tm, d), lambda j, i: (i, 0)),
            pl.BlockSpec((d, tn), lambda j, i: (0, j)),
            pl.BlockSpec((2, nk, half), lambda j, i: (0, 0, 0)),
        ],
        out_specs=pl.BlockSpec((gpt, nk, tm), lambda j, i: (j, 0, i)),
        out_shape=jax.ShapeDtypeStruct((2 * n_heads, nk, t), jnp.float32),
        scratch_shapes=[pltpu.VMEM((d, tn), jnp.bfloat16)],
        compiler_params=_cparams("parallel", "arbitrary"),
        name="peer_scores",
    )(xn, wq, subkeys)


def _row_ids(rows):
    return lax.broadcasted_iota(jnp.int32, (rows, LANES), 0).astype(jnp.float32)


def _extract_top(cur_ref, count):
    rows, width = cur_ref.shape
    ntile = width // LANES
    row = _row_ids(rows)
    slot = _row_ids(count)

    def body(r, carry):
        vals, firsts = carry
        rf = lax.convert_element_type(r, jnp.float32)
        new_vals, new_firsts = [], []
        for tile in range(ntile):
            sl = slice(tile * LANES, (tile + 1) * LANES)
            cur = cur_ref[:, sl]
            m = jnp.max(cur, axis=0, keepdims=True)
            first = jnp.min(jnp.where(cur == m, row, float(rows)), axis=0, keepdims=True)
            cur_ref[:, sl] = jnp.where(row == first, NEG_INF, cur)
            new_vals.append(jnp.where(slot == rf, m, vals[:, sl]))
            new_firsts.append(jnp.where(slot == rf, first, firsts[:, sl]))
        return jnp.concatenate(new_vals, axis=1), jnp.concatenate(new_firsts, axis=1)

    init = jnp.zeros((count, width), jnp.float32)
    return lax.fori_loop(0, count, body, (init, init))


def _extract_distinct(cur_ref, count):
    rows, width = cur_ref.shape
    slot = _row_ids(count)

    def body(r, vals):
        rf = lax.convert_element_type(r, jnp.float32)
        new_vals = []
        for tile in range(width // LANES):
            sl = slice(tile * LANES, (tile + 1) * LANES)
            cur = cur_ref[:, sl]
            m = jnp.max(cur, axis=0, keepdims=True)
            cur_ref[:, sl] = jnp.where(cur == m, NEG_INF, cur)
            new_vals.append(jnp.where(slot == rf, m, vals[:, sl]))
        return jnp.concatenate(new_vals, axis=1)

    return lax.fori_loop(0, count, body, jnp.zeros((count, width), jnp.float32))


def _removed(cur_ref):
    return jnp.sum(jnp.where(cur_ref[...] == NEG_INF, 1.0, 0.0), axis=0, keepdims=True)


PAIR_BASE = (SUBLANES + 1) * SUBLANES
PAIR_ROWS = PAIR_BASE + SUBLANES
PAIR_COUNT = sum(PEER_TOPK // (i + 1) for i in range(PEER_TOPK))


def _fill_pairs(cand_ref, a, b):
    k = PEER_TOPK
    sub = lax.broadcasted_iota(jnp.int32, (SUBLANES, a.shape[1]), 0)
    cand_ref[0:SUBLANES] = a[0:1] + b[0:SUBLANES]
    cand_ref[SUBLANES:2 * SUBLANES] = a[0:1] + b[SUBLANES:2 * SUBLANES]
    for i in range(1, SUBLANES):
        cand_ref[(i + 1) * SUBLANES:(i + 2) * SUBLANES] = jnp.where(
            sub < k // (i + 1), a[i:i + 1] + b[0:SUBLANES], NEG_INF)
    cand_ref[PAIR_BASE:PAIR_ROWS] = a[SUBLANES:] + b[0:1]


def _topk_kernel(s_ref, r2_ref, e2_ref, ell_ref, w_ref, cur_ref, cand_ref):
    k = PEER_TOPK
    nchunk = r2_ref.shape[1]
    nk = s_ref.shape[1]
    tl = nchunk * LANES
    cur_ref[:, :tl] = s_ref[0]
    cur_ref[:, tl:] = s_ref[1]
    vals = _extract_distinct(cur_ref, k)
    gone = _removed(cur_ref)
    a = vals[:, :tl]
    b = vals[:, tl:]
    _fill_pairs(cand_ref, a, b)
    cvals = _extract_distinct(cand_ref, k)
    expect_pairs = float(PAIR_ROWS - PAIR_COUNT + k)
    ties = jnp.max(jnp.abs(gone[:, :tl] - float(k)) + jnp.abs(gone[:, tl:] - float(k))
                   + jnp.abs(_removed(cand_ref) - expect_pairs))

    def emit(a, b, cvals, lims, r2_of, ell_of):
        z = jnp.sum(jnp.exp(cvals - cvals[0:1]), axis=0, keepdims=True)
        e2 = jnp.exp(s_ref[1] - b[0:1])
        w = jnp.exp(s_ref[0] - a[0:1]) * (0.5 / z)
        for c in range(nchunk):
            sl = slice(c * LANES, (c + 1) * LANES)
            r2 = jnp.full((nk, LANES), float(k), jnp.float32)
            ell = jnp.zeros((nk, LANES), jnp.float32)
            for r in range(k - 1, -1, -1):
                r2 = jnp.where(r2_of(r, c), float(r), r2)
                ell = jnp.where(ell_of(r, c), lims[r][:, sl], ell)
            r2_ref[0, c] = r2
            e2_ref[0, c] = e2[:, sl]
            ell_ref[0, c] = ell
            w_ref[0, c] = w[:, sl]

    @pl.when(ties == 0.0)
    def _():
        took = jnp.where(cand_ref[...] == NEG_INF, 1.0, 0.0)
        lims = [jnp.sum(took[0:2 * SUBLANES], axis=0, keepdims=True)]
        for i in range(1, SUBLANES):
            pad = float(SUBLANES - k // (i + 1))
            lims.append(jnp.sum(took[(i + 1) * SUBLANES:(i + 2) * SUBLANES], axis=0, keepdims=True) - pad)
        lims += [took[PAIR_BASE + j:PAIR_BASE + j + 1] for j in range(SUBLANES)]

        def r2_of(r, c):
            return s_ref[1, :, c * LANES:(c + 1) * LANES] == b[r:r + 1, c * LANES:(c + 1) * LANES]

        def ell_of(r, c):
            return s_ref[0, :, c * LANES:(c + 1) * LANES] == a[r:r + 1, c * LANES:(c + 1) * LANES]

        emit(a, b, cvals, lims, r2_of, ell_of)

    @pl.when(ties != 0.0)
    def _():
        cur_ref[:, :tl] = s_ref[0]
        cur_ref[:, tl:] = s_ref[1]
        vals, firsts = _extract_top(cur_ref, k)
        a = vals[:, :tl]
        b = vals[:, tl:]
        _fill_pairs(cand_ref, a, b)
        cvals, cfirsts = _extract_top(cand_ref, k)
        pair_i = jnp.where(cfirsts < float(2 * SUBLANES), 0.0,
                           jnp.where(cfirsts < float(PAIR_BASE),
                                     jnp.floor(cfirsts * (1.0 / SUBLANES)) - 1.0,
                                     cfirsts - float(PAIR_BASE - SUBLANES)))
        lims = [jnp.sum(jnp.where(pair_i == float(i), 1.0, 0.0), axis=0, keepdims=True)
                for i in range(k)]
        row = _row_ids(nk)

        def r2_of(r, c):
            return row == firsts[r:r + 1, tl + c * LANES:tl + (c + 1) * LANES]

        def ell_of(r, c):
            return row == firsts[r:r + 1, c * LANES:(c + 1) * LANES]

        emit(a, b, cvals, lims, r2_of, ell_of)


def peer_topk(scores, n_heads, *, tl=512):
    hs, nk, t = scores.shape
    assert hs == 2 * n_heads and PEER_TOPK == 2 * SUBLANES
    tl = min(tl, t)
    assert t % tl == 0 and tl % LANES == 0
    nchunk = tl // LANES
    ncand = (SUBLANES + 2) * SUBLANES
    shape = (n_heads, t // LANES, nk, LANES)
    ospec = pl.BlockSpec((1, nchunk, nk, LANES), lambda h, i: (h, i, 0, 0))
    return pl.pallas_call(
        _topk_kernel,
        grid=(n_heads, t // tl),
        in_specs=[pl.BlockSpec((2, nk, tl), lambda h, i: (h, 0, i))],
        out_specs=[ospec] * 4,
        out_shape=[jax.ShapeDtypeStruct(shape, jnp.float32)] * 4,
        scratch_shapes=[pltpu.VMEM((nk, 2 * tl), jnp.float32), pltpu.VMEM((ncand, tl), jnp.float32)],
        compiler_params=_cparams("parallel", "parallel"),
        name="peer_topk",
    )(scores)


def _peer_hidden_kernel(u_ref, xn_ref, o_ref, ub_ref):
    @pl.when(pl.program_id(1) == 0)
    def _():
        ub_ref[...] = u_ref[...].astype(jnp.bfloat16)

    hid = lax.dot_general(ub_ref[...], xn_ref[...], NT_DIMS, preferred_element_type=jnp.float32)
    o_ref[...] = hid * (1.0 + lax.erf(hid * INV_SQRT2))


def peer_hidden(xn, u, *, tb=1024, eb=1024):
    t, d = xn.shape
    e = u.shape[0]
    tb = min(tb, t)
    assert t % tb == 0 and e % eb == 0
    return pl.pallas_call(
        _peer_hidden_kernel,
        grid=(e // eb, t // tb),
        in_specs=[
            pl.BlockSpec((eb, d), lambda j, i: (j, 0)),
            pl.BlockSpec((tb, d), lambda j, i: (i, 0)),
        ],
        out_specs=pl.BlockSpec((eb, tb), lambda j, i: (j, i)),
        out_shape=jax.ShapeDtypeStruct((e, t), jnp.float32),
        scratch_shapes=[pltpu.VMEM((eb, d), jnp.bfloat16)],
        compiler_params=_cparams("parallel", "arbitrary"),
        name="peer_hidden",
    )(u, xn)


GATE_GROUP = 4


def _peer_dense_kernel(h_ref, vt_ref, r2_ref, e2_ref, ell_ref, w_ref, x_ref, gain_ref, o_ref,
                       acc_ref, a_ref, *, final_norm):
    n_heads, nchunk, nk, _ = r2_ref.shape
    na = ell_ref.shape[2]
    j = pl.program_id(1)

    @pl.when(j == 0)
    def _():
        acc_ref[...] = jnp.zeros_like(acc_ref)

    hb = nk // 2
    bf = jnp.bfloat16
    for a0 in range(0, na, GATE_GROUP):
        group = range(a0, a0 + GATE_GROUP)
        for c in range(nchunk):
            lanes = slice(c * LANES, (c + 1) * LANES)
            for half in range(2):
                keys = slice(half * hb, (half + 1) * hb)
                g = {ai: None for ai in group}
                for h in range(n_heads):
                    r2 = r2_ref[h, c, keys, :].astype(bf)
                    e2 = e2_ref[h, c, keys, :].astype(bf)
                    for ai in group:
                        lim = jnp.broadcast_to(ell_ref[h, c, ai:ai + 1, :], (hb, LANES)).astype(bf)
                        wa = jnp.broadcast_to(w_ref[h, c, ai:ai + 1, :], (hb, LANES)).astype(bf)
                        term = jnp.where(r2 < lim, e2, jnp.zeros_like(e2)) * wa
                        g[ai] = term if g[ai] is None else g[ai] + term
                for ai in group:
                    rows = slice(ai * nk + half * hb, ai * nk + (half + 1) * hb)
                    act = h_ref[rows, lanes]
                    a_ref[rows, lanes] = act.astype(bf) * g[ai]
    acc_ref[...] += jnp.dot(vt_ref[...], a_ref[...], preferred_element_type=jnp.float32)

    @pl.when(j == pl.num_programs(1) - 1)
    def _():
        x = x_ref[...] + acc_ref[...].T
        if final_norm:
            inv = lax.rsqrt(jnp.mean(x * x, axis=-1, keepdims=True) + RMS_EPS)
            x = x * inv * gain_ref[...]
        o_ref[...] = x


def peer_dense(act_t, vt, r2, e2, ell, w, x, gain, *, final_norm, tb=512, na=8):
    e, t = act_t.shape
    d = vt.shape[0]
    n_heads, _, nk, _ = r2.shape
    eb = na * nk
    tb = min(tb, t)
    assert t % tb == 0 and tb % LANES == 0 and e % eb == 0 and e == nk * nk and na % SUBLANES == 0
    nchunk = tb // LANES
    key_spec = pl.BlockSpec((n_heads, nchunk, nk, LANES), lambda i, j: (0, i, 0, 0))
    row_spec = pl.BlockSpec((n_heads, nchunk, na, LANES), lambda i, j: (0, i, j, 0))
    return pl.pallas_call(
        functools.partial(_peer_dense_kernel, final_norm=final_norm),
        grid=(t // tb, e // eb),
        in_specs=[
            pl.BlockSpec((eb, tb), lambda i, j: (j, i)),
            pl.BlockSpec((d, eb), lambda i, j: (0, j)),
            key_spec, key_spec, row_spec, row_spec,
            pl.BlockSpec((tb, d), lambda i, j: (i, 0)),
            pl.BlockSpec((1, d), lambda i, j: (0, 0)),
        ],
        out_specs=pl.BlockSpec((tb, d), lambda i, j: (i, 0)),
        out_shape=jax.ShapeDtypeStruct((t, d), jnp.float32),
        scratch_shapes=[pltpu.VMEM((d, tb), jnp.float32), pltpu.VMEM((eb, tb), jnp.bfloat16)],
        compiler_params=_cparams("parallel", "arbitrary"),
        name="peer_dense",
    )(act_t, vt, r2, e2, ell, w, x, gain.reshape(1, d))


ATT_BLK = 512


def _layer(x, norm_mix, w_in, conv_w, w_conv_out, w_attn_out, w_out, norm_ffn,
           peer_wq, peer_subkeys, peer_u, peer_v, norm_final, is_last):
    t, d = x.shape
    c = conv_w.shape[1]
    a = w_attn_out.shape[0]
    n_keys, half = peer_subkeys.shape[1:]
    peer_heads = peer_wq.shape[1] // (2 * half)
    dh = LANES // 2
    attn_heads = a // dh
    bf = jnp.bfloat16

    xn = rmsnorm(x, norm_mix, bf)
    pc = matmul(xn, w_in, 0, 3 * c, jnp.float32)
    q = matmul(xn, w_in, 3 * c, a, bf, scale=dh ** -0.5)
    k = matmul(xn, w_in, 3 * c + a, a, bf)
    vt = matmul_nt(w_in[:, 3 * c + 2 * a:3 * c + 3 * a].T.astype(bf), xn, bf, tm=min(ATT_BLK, t))
    gates = matmul(xn, w_in, 3 * c + 3 * a, 2 * d, bf, sigmoid=True)

    yc = conv_mixer(pc, conv_w)
    ya = attention(q, k, vt, attn_heads, dh, blk=min(ATT_BLK, t))
    merged = merge(yc, ya, w_conv_out.astype(bf), w_attn_out.astype(bf), gates)
    x1, xn2 = out_proj(merged, w_out.astype(bf), x, norm_ffn)
    scores = peer_scores(xn2, peer_wq, peer_subkeys.astype(bf), peer_heads)
    r2, e2, ell, w = peer_topk(scores, peer_heads)
    act_t = peer_hidden(xn2, peer_u)
    return peer_dense(act_t, peer_v.T.astype(bf), r2, e2, ell, w, x1, norm_final, final_norm=is_last)


def kernel(x, norm_mix, w_in, conv_w, w_conv_out, w_attn_out, w_out, norm_ffn, peer_wq,
           peer_subkeys, peer_u, peer_v, norm_final):
    b, t, d = x.shape
    depth = norm_mix.shape[0]
    outs = []
    for bi in range(b):
        xb = x[bi]
        for layer in range(depth):
            xb = _layer(xb, norm_mix[layer], w_in[layer], conv_w[layer], w_conv_out[layer],
                        w_attn_out[layer], w_out[layer], norm_ffn[layer], peer_wq[layer],
                        peer_subkeys[layer], peer_u[layer], peer_v[layer], norm_final,
                        is_last=layer == depth - 1)
        outs.append(xb)
    return jnp.stack(outs, axis=0)
```

```python
import functools

import jax
import jax.numpy as jnp
from jax import lax
from jax.experimental import pallas as pl
from jax.experimental.pallas import tpu as pltpu

RMS_EPS = 1e-6
LANES = 128
SUBLANES = 8
BF16_ROWS = 16
VMEM_LIMIT = 56 * 1024 * 1024
PEER_TOPK = 16
NEG_INF = float("-inf")
INV_SQRT2 = 0.7071067811865476
LOG2E = 1.4426950408889634
NT_DIMS = (((1,), (1,)), ((), ()))


def _cparams(*sem):
    return pltpu.CompilerParams(dimension_semantics=sem, vmem_limit_bytes=VMEM_LIMIT)


def _rmsnorm_kernel(x_ref, g_ref, o_ref):
    x = x_ref[...]
    inv = lax.rsqrt(jnp.mean(x * x, axis=-1, keepdims=True) + RMS_EPS)
    o_ref[...] = (x * inv * g_ref[...]).astype(o_ref.dtype)


def rmsnorm(x, gain, out_dtype, *, tm=512):
    t, d = x.shape
    tm = min(tm, t)
    assert t % tm == 0
    return pl.pallas_call(
        _rmsnorm_kernel,
        grid=(t // tm,),
        in_specs=[pl.BlockSpec((tm, d), lambda i: (i, 0)), pl.BlockSpec((1, d), lambda i: (0, 0))],
        out_specs=pl.BlockSpec((tm, d), lambda i: (i, 0)),
        out_shape=jax.ShapeDtypeStruct((t, d), out_dtype),
        compiler_params=_cparams("parallel"),
        name="rmsnorm",
    )(x, gain.reshape(1, d))


def _matmul_kernel(x_ref, w_ref, o_ref, wb_ref, *, scale, sigmoid):
    @pl.when(pl.program_id(1) == 0)
    def _():
        wb_ref[...] = w_ref[...].astype(jnp.bfloat16)

    acc = jnp.dot(x_ref[...], wb_ref[...], preferred_element_type=jnp.float32)
    if scale != 1.0:
        acc = acc * scale
    if sigmoid:
        acc = jax.nn.sigmoid(acc)
    o_ref[...] = acc.astype(o_ref.dtype)


def matmul(x, w, col0, ncols, out_dtype, *, scale=1.0, sigmoid=False, tm=1024, tn=1024):
    t, k = x.shape
    tm = min(tm, t)
    tn = min(tn, ncols)
    assert t % tm == 0 and ncols % tn == 0 and col0 % tn == 0
    jb0 = col0 // tn
    return pl.pallas_call(
        functools.partial(_matmul_kernel, scale=scale, sigmoid=sigmoid),
        grid=(ncols // tn, t // tm),
        in_specs=[
            pl.BlockSpec((tm, k), lambda j, i: (i, 0)),
            pl.BlockSpec((k, tn), lambda j, i: (0, jb0 + j)),
        ],
        out_specs=pl.BlockSpec((tm, tn), lambda j, i: (i, j)),
        out_shape=jax.ShapeDtypeStruct((t, ncols), out_dtype),
        scratch_shapes=[pltpu.VMEM((k, tn), jnp.bfloat16)],
        compiler_params=_cparams("parallel", "arbitrary"),
        name="matmul",
    )(x, w)


def _matmul_nt_kernel(wt_ref, x_ref, o_ref):
    o_ref[0] = lax.dot_general(wt_ref[...], x_ref[...], NT_DIMS,
                               preferred_element_type=jnp.float32).astype(o_ref.dtype)


def matmul_nt(wt, x, out_dtype, *, tm, tn=1024):
    t, k = x.shape
    n = wt.shape[0]
    tn = min(tn, n)
    assert t % tm == 0 and n % tn == 0
    return pl.pallas_call(
        _matmul_nt_kernel,
        grid=(t // tm, n // tn),
        in_specs=[
            pl.BlockSpec((tn, k), lambda i, j: (j, 0)),
            pl.BlockSpec((tm, k), lambda i, j: (i, 0)),
        ],
        out_specs=pl.BlockSpec((1, tn, tm), lambda i, j: (i, j, 0)),
        out_shape=jax.ShapeDtypeStruct((t // tm, n, tm), out_dtype),
        compiler_params=_cparams("parallel", "parallel"),
        name="matmul_nt",
    )(wt, x)


def _conv_kernel(cb_ref, cc_ref, ch_ref, cch_ref, chh_ref, w_ref, o_ref, z_ref):
    tm = cb_ref.shape[0]
    z_halo = cch_ref[...] * chh_ref[...]
    z_ref[0:SUBLANES, :] = jnp.where(pl.program_id(0) == 0, 0.0, z_halo)
    z = cc_ref[...] * ch_ref[...]
    z_ref[SUBLANES:, :] = z
    w = w_ref[...]
    k = w.shape[0]
    y = z * w[k - 1:k, :]
    for j in range(1, k):
        y = y + z_ref[pl.ds(SUBLANES - j, tm), :] * w[k - 1 - j:k - j, :]
    o_ref[...] = (cb_ref[...] * y).astype(o_ref.dtype)


def conv_mixer(pc, conv_w, *, tm=512):
    t, c3 = pc.shape
    c = c3 // 3
    k = conv_w.shape[0]
    assert k - 1 <= SUBLANES
    tm = min(tm, t)
    assert t % tm == 0 and tm % SUBLANES == 0
    hb = tm // SUBLANES

    def halo_map(col):
        return lambda i: (jnp.maximum(i * hb - 1, 0), col)

    return pl.pallas_call(
        _conv_kernel,
        grid=(t // tm,),
        in_specs=[
            pl.BlockSpec((tm, c), lambda i: (i, 0)),
            pl.BlockSpec((tm, c), lambda i: (i, 1)),
            pl.BlockSpec((tm, c), lambda i: (i, 2)),
            pl.BlockSpec((SUBLANES, c), halo_map(1)),
            pl.BlockSpec((SUBLANES, c), halo_map(2)),
            pl.BlockSpec((k, c), lambda i: (0, 0)),
        ],
        out_specs=pl.BlockSpec((tm, c), lambda i: (i, 0)),
        out_shape=jax.ShapeDtypeStruct((t, c), jnp.bfloat16),
        scratch_shapes=[pltpu.VMEM((tm + SUBLANES, c), jnp.float32)],
        compiler_params=_cparams("parallel"),
        name="conv_mixer",
    )(pc, pc, pc, pc, pc, conv_w)


ATT_SUB = 128


def _attn_kernel(q_ref, k_ref, vt_ref, tri_ref, o_ref, acc_ref, *, dh):
    blk = q_ref.shape[0]
    nheads = 2 * (q_ref.shape[1] // LANES)
    nsub = blk // ATT_SUB
    i = pl.program_id(1)
    lane = lax.broadcasted_iota(jnp.int32, (1, LANES), 1)
    head0 = lane < dh
    qh = []
    for g in range(nheads // 2):
        q2 = q_ref[:, g * LANES:(g + 1) * LANES]
        zero = jnp.zeros_like(q2)
        qh += [jnp.where(head0, q2, zero), jnp.where(head0, zero, q2)]
    tri = tri_ref[...]
    acc_ref[...] = jnp.zeros_like(acc_ref)

    def block(jb, cars, diagonal):
        kb = k_ref[pl.ds(pl.multiple_of(jb * blk, blk), blk), :]
        vt = vt_ref[jb]
        new_cars = []
        zts = [lax.dot_general(kb[:, (h // 2) * LANES:(h // 2 + 1) * LANES], qh[h], NT_DIMS,
                               preferred_element_type=jnp.float32)
               for h in range(nheads)]
        for h in range(nheads):
            car = cars[h]
            ws = [None] * nsub
            for s in range(nsub - 1, -1, -1):
                z = zts[h][s * ATT_SUB:(s + 1) * ATT_SUB]
                sp = jnp.maximum(z, 0.0) + jnp.log(1.0 + jnp.exp2(jnp.abs(z) * -LOG2E))
                logsig = z - sp
                if diagonal:
                    key = lax.broadcasted_iota(jnp.int32, z.shape, 0) + s * ATT_SUB
                    qry = lax.broadcasted_iota(jnp.int32, z.shape, 1)
                    mask = key < qry
                    sp = jnp.where(mask, sp, 0.0)
                r = jnp.dot(tri, sp.astype(jnp.bfloat16), preferred_element_type=jnp.float32)
                sfx = r[:ATT_SUB]
                tot = r[ATT_SUB:ATT_SUB + 1]
                w = jnp.exp(logsig - sfx - car)
                if diagonal:
                    w = jnp.where(mask, w, 0.0)
                ws[s] = w.astype(jnp.bfloat16)
                car = car + tot
            wt = jnp.concatenate(ws, axis=0)
            acc_ref[h * dh:(h + 1) * dh, :] += jnp.dot(vt[h * dh:(h + 1) * dh, :], wt,
                                                       preferred_element_type=jnp.float32)
            new_cars.append(car)
        return tuple(new_cars)

    car0 = jnp.zeros((1, blk), jnp.float32)
    cars = block(i, (car0,) * nheads, True)

    def body(n, cars):
        return block(i - 1 - n, cars, False)

    lax.fori_loop(0, i, body, cars)
    o_ref[...] = acc_ref[...].T.astype(o_ref.dtype)


def _tri_matrix():
    s = jnp.arange(ATT_SUB)[:, None]
    j = jnp.arange(ATT_SUB)[None, :]
    later = (j > s).astype(jnp.bfloat16)
    ones = jnp.ones((BF16_ROWS, ATT_SUB), jnp.bfloat16)
    return jnp.concatenate([later, ones], axis=0)


def attention(q, k, vt, n_heads, dh, *, blk=512, heads_per_step=8):
    t = q.shape[0]
    heads_per_step = min(heads_per_step, n_heads)
    assert 2 * dh == LANES and heads_per_step % 2 == 0 and dh % BF16_ROWS == 0
    assert n_heads % heads_per_step == 0
    assert t % blk == 0 and blk % ATT_SUB == 0 and vt.shape == (t // blk, n_heads * dh, blk)
    ngroup = n_heads // heads_per_step
    gw = heads_per_step * dh
    nb = t // blk
    return pl.pallas_call(
        functools.partial(_attn_kernel, dh=dh),
        grid=(ngroup, nb),
        in_specs=[
            pl.BlockSpec((blk, gw), lambda p, i: (i, p)),
            pl.BlockSpec((t, gw), lambda p, i: (0, p), pipeline_mode=pl.Buffered(1)),
            pl.BlockSpec((nb, gw, blk), lambda p, i: (0, p, 0), pipeline_mode=pl.Buffered(1)),
            pl.BlockSpec((ATT_SUB + BF16_ROWS, ATT_SUB), lambda p, i: (0, 0)),
        ],
        out_specs=pl.BlockSpec((blk, gw), lambda p, i: (i, p)),
        out_shape=jax.ShapeDtypeStruct((t, n_heads * dh), jnp.bfloat16),
        scratch_shapes=[pltpu.VMEM((gw, blk), jnp.float32)],
        compiler_params=_cparams("parallel", "parallel"),
        name="stickbreak_attention",
    )(q, k, vt, _tri_matrix())


def _merge_kernel(yc_ref, ya_ref, wc_ref, wa_ref, gc_ref, ga_ref, o_ref):
    yc = jnp.dot(yc_ref[...], wc_ref[...], preferred_element_type=jnp.float32)
    ya = jnp.dot(ya_ref[...], wa_ref[...], preferred_element_type=jnp.float32)
    o_ref[...] = (gc_ref[...].astype(jnp.float32) * yc
                  + ga_ref[...].astype(jnp.float32) * ya).astype(o_ref.dtype)


def merge(yc, ya, wc, wa, gates, *, tm=1024, tn=1024):
    t, c = yc.shape
    a = ya.shape[1]
    d = wc.shape[1]
    tm = min(tm, t)
    tn = min(tn, d)
    assert t % tm == 0 and d % tn == 0
    nj = d // tn
    return pl.pallas_call(
        _merge_kernel,
        grid=(t // tm, nj),
        in_specs=[
            pl.BlockSpec((tm, c), lambda i, j: (i, 0)),
            pl.BlockSpec((tm, a), lambda i, j: (i, 0)),
            pl.BlockSpec((c, tn), lambda i, j: (0, j)),
            pl.BlockSpec((a, tn), lambda i, j: (0, j)),
            pl.BlockSpec((tm, tn), lambda i, j: (i, j)),
            pl.BlockSpec((tm, tn), lambda i, j: (i, nj + j)),
        ],
        out_specs=pl.BlockSpec((tm, tn), lambda i, j: (i, j)),
        out_shape=jax.ShapeDtypeStruct((t, d), jnp.bfloat16),
        compiler_params=_cparams("parallel", "parallel"),
        name="merge_mixers",
    )(yc, ya, wc, wa, gates, gates)


def _out_proj_kernel(m_ref, w_ref, x_ref, g_ref, o_ref, xn_ref):
    x1 = x_ref[...] + jnp.dot(m_ref[...], w_ref[...], preferred_element_type=jnp.float32)
    o_ref[...] = x1
    inv = lax.rsqrt(jnp.mean(x1 * x1, axis=-1, keepdims=True) + RMS_EPS)
    xn_ref[...] = (x1 * inv * g_ref[...]).astype(xn_ref.dtype)


def out_proj(merged, w, x, gain, *, tm=512):
    t, d = merged.shape
    n = w.shape[1]
    tm = min(tm, t)
    assert t % tm == 0 and x.shape == (t, n)
    return pl.pallas_call(
        _out_proj_kernel,
        grid=(t // tm,),
        in_specs=[
            pl.BlockSpec((tm, d), lambda i: (i, 0)),
            pl.BlockSpec((d, n), lambda i: (0, 0), pipeline_mode=pl.Buffered(1)),
            pl.BlockSpec((tm, n), lambda i: (i, 0)),
            pl.BlockSpec((1, n), lambda i: (0, 0)),
        ],
        out_specs=[pl.BlockSpec((tm, n), lambda i: (i, 0)), pl.BlockSpec((tm, n), lambda i: (i, 0))],
        out_shape=[jax.ShapeDtypeStruct((t, n), jnp.float32), jax.ShapeDtypeStruct((t, n), jnp.bfloat16)],
        compiler_params=_cparams("parallel"),
        name="out_proj",
    )(merged, w, x, gain.reshape(1, n))


def _scores_kernel(x_ref, w_ref, k_ref, o_ref, wb_ref):
    @pl.when(pl.program_id(1) == 0)
    def _():
        wb_ref[...] = w_ref[...].astype(jnp.bfloat16)

    half = k_ref.shape[2]
    q = jnp.dot(x_ref[...], wb_ref[...], preferred_element_type=jnp.float32)
    for g in range(o_ref.shape[0]):
        qg = q[:, g * half:(g + 1) * half].astype(jnp.bfloat16)
        keys = k_ref[g % 2]
        o_ref[g] = lax.dot_general(keys, qg, NT_DIMS, preferred_element_type=jnp.float32)


def peer_scores(xn, wq, subkeys, n_heads, *, tm=1024, tn=1024):
    t, d = xn.shape
    _, nk, half = subkeys.shape
    qd = wq.shape[1]
    assert qd == n_heads * 2 * half
    tm = min(tm, t)
    tn = min(tn, qd)
    assert t % tm == 0 and qd % tn == 0 and tn % (2 * half) == 0
    gpt = tn // half
    return pl.pallas_call(
        _scores_kernel,
        grid=(qd // tn, t // tm),
        in_specs=[
            pl.BlockSpec((tm, d), lambda j, i: (i, 0)),
            pl.BlockSpec((d, tn), lambda j, i: (0, j)),
            pl.BlockSpec((2, nk, half), lambda j, i: (0, 0, 0)),
        ],
        out_specs=pl.BlockSpec((gpt, nk, tm), lambda j, i: (j, 0, i)),
        out_shape=jax.ShapeDtypeStruct((2 * n_heads, nk, t), jnp.float32),
        scratch_shapes=[pltpu.VMEM((d, tn), jnp.bfloat16)],
        compiler_params=_cparams("parallel", "arbitrary"),
        name="peer_scores",
    )(xn, wq, subkeys)


def _row_ids(rows):
    return lax.broadcasted_iota(jnp.int32, (rows, LANES), 0).astype(jnp.float32)


def _extract_top(cur_ref, count):
    rows, width = cur_ref.shape
    ntile = width // LANES
    row = _row_ids(rows)
    slot = _row_ids(count)

    def body(r, carry):
        vals, firsts = carry
        rf = lax.convert_element_type(r, jnp.float32)
        new_vals, new_firsts = [], []
        for tile in range(ntile):
            sl = slice(tile * LANES, (tile + 1) * LANES)
            cur = cur_ref[:, sl]
            m = jnp.max(cur, axis=0, keepdims=True)
            first = jnp.min(jnp.where(cur == m, row, float(rows)), axis=0, keepdims=True)
            cur_ref[:, sl] = jnp.where(row == first, NEG_INF, cur)
            new_vals.append(jnp.where(slot == rf, m, vals[:, sl]))
            new_firsts.append(jnp.where(slot == rf, first, firsts[:, sl]))
        return jnp.concatenate(new_vals, axis=1), jnp.concatenate(new_firsts, axis=1)

    init = jnp.zeros((count, width), jnp.float32)
    return lax.fori_loop(0, count, body, (init, init))


def _extract_distinct(cur_ref, count):
    rows, width = cur_ref.shape
    slot = _row_ids(count)

    def body(r, vals):
        rf = lax.convert_element_type(r, jnp.float32)
        new_vals = []
        for tile in range(width // LANES):
            sl = slice(tile * LANES, (tile + 1) * LANES)
            cur = cur_ref[:, sl]
            m = jnp.max(cur, axis=0, keepdims=True)
            cur_ref[:, sl] = jnp.where(cur == m, NEG_INF, cur)
            new_vals.append(jnp.where(slot == rf, m, vals[:, sl]))
        return jnp.concatenate(new_vals, axis=1)

    return lax.fori_loop(0, count, body, jnp.zeros((count, width), jnp.float32))


def _removed(cur_ref):
    return jnp.sum(jnp.where(cur_ref[...] == NEG_INF, 1.0, 0.0), axis=0, keepdims=True)


PAIR_BASE = (SUBLANES + 1) * SUBLANES
PAIR_ROWS = PAIR_BASE + SUBLANES
PAIR_COUNT = sum(PEER_TOPK // (i + 1) for i in range(PEER_TOPK))


def _fill_pairs(cand_ref, a, b):
    k = PEER_TOPK
    sub = lax.broadcasted_iota(jnp.int32, (SUBLANES, a.shape[1]), 0)
    cand_ref[0:SUBLANES] = a[0:1] + b[0:SUBLANES]
    cand_ref[SUBLANES:2 * SUBLANES] = a[0:1] + b[SUBLANES:2 * SUBLANES]
    for i in range(1, SUBLANES):
        cand_ref[(i + 1) * SUBLANES:(i + 2) * SUBLANES] = jnp.where(
            sub < k // (i + 1), a[i:i + 1] + b[0:SUBLANES], NEG_INF)
    cand_ref[PAIR_BASE:PAIR_ROWS] = a[SUBLANES:] + b[0:1]


def _topk_kernel(s_ref, r2_ref, e2_ref, ell_ref, w_ref, cur_ref, cand_ref):
    k = PEER_TOPK
    nchunk = r2_ref.shape[1]
    nk = s_ref.shape[1]
    tl = nchunk * LANES
    cur_ref[:, :tl] = s_ref[0]
    cur_ref[:, tl:] = s_ref[1]
    vals = _extract_distinct(cur_ref, k)
    gone = _removed(cur_ref)
    a = vals[:, :tl]
    b = vals[:, tl:]
    _fill_pairs(cand_ref, a, b)
    cvals = _extract_distinct(cand_ref, k)
    expect_pairs = float(PAIR_ROWS - PAIR_COUNT + k)
    ties = jnp.max(jnp.abs(gone[:, :tl] - float(k)) + jnp.abs(gone[:, tl:] - float(k))
                   + jnp.abs(_removed(cand_ref) - expect_pairs))

    def emit(a, b, cvals, lims, r2_of, ell_of):
        z = jnp.sum(jnp.exp(cvals - cvals[0:1]), axis=0, keepdims=True)
        e2 = jnp.exp(s_ref[1] - b[0:1])
        w = jnp.exp(s_ref[0] - a[0:1]) * (0.5 / z)
        for c in range(nchunk):
            sl = slice(c * LANES, (c + 1) * LANES)
            r2 = jnp.full((nk, LANES), float(k), jnp.float32)
            ell = jnp.zeros((nk, LANES), jnp.float32)
            for r in range(k - 1, -1, -1):
                r2 = jnp.where(r2_of(r, c), float(r), r2)
                ell = jnp.where(ell_of(r, c), lims[r][:, sl], ell)
            r2_ref[0, c] = r2
            e2_ref[0, c] = e2[:, sl]
            ell_ref[0, c] = ell
            w_ref[0, c] = w[:, sl]

    @pl.when(ties == 0.0)
    def _():
        took = jnp.where(cand_ref[...] == NEG_INF, 1.0, 0.0)
        lims = [jnp.sum(took[0:2 * SUBLANES], axis=0, keepdims=True)]
        for i in range(1, SUBLANES):
            pad = float(SUBLANES - k // (i + 1))
            lims.append(jnp.sum(took[(i + 1) * SUBLANES:(i + 2) * SUBLANES], axis=0, keepdims=True) - pad)
        lims += [took[PAIR_BASE + j:PAIR_BASE + j + 1] for j in range(SUBLANES)]

        def r2_of(r, c):
            return s_ref[1, :, c * LANES:(c + 1) * LANES] == b[r:r + 1, c * LANES:(c + 1) * LANES]

        def ell_of(r, c):
            return s_ref[0, :, c * LANES:(c + 1) * LANES] == a[r:r + 1, c * LANES:(c + 1) * LANES]

        emit(a, b, cvals, lims, r2_of, ell_of)

    @pl.when(ties != 0.0)
    def _():
        cur_ref[:, :tl] = s_ref[0]
        cur_ref[:, tl:] = s_ref[1]
        vals, firsts = _extract_top(cur_ref, k)
        a = vals[:, :tl]
        b = vals[:, tl:]
        _fill_pairs(cand_ref, a, b)
        cvals, cfirsts = _extract_top(cand_ref, k)
        pair_i = jnp.where(cfirsts < float(2 * SUBLANES), 0.0,
                           jnp.where(cfirsts < float(PAIR_BASE),
                                     jnp.floor(cfirsts * (1.0 / SUBLANES)) - 1.0,
                                     cfirsts - float(PAIR_BASE - SUBLANES)))
        lims = [jnp.sum(jnp.where(pair_i == float(i), 1.0, 0.0), axis=0, keepdims=True)
                for i in range(k)]
        row = _row_ids(nk)

        def r2_of(r, c):
            return row == firsts[r:r + 1, tl + c * LANES:tl + (c + 1) * LANES]

        def ell_of(r, c):
            return row == firsts[r:r + 1, c * LANES:(c + 1) * LANES]

        emit(a, b, cvals, lims, r2_of, ell_of)


def peer_topk(scores, n_heads, *, tl=512):
    hs, nk, t = scores.shape
    assert hs == 2 * n_heads and PEER_TOPK == 2 * SUBLANES
    tl = min(tl, t)
    assert t % tl == 0 and tl % LANES == 0
    nchunk = tl // LANES
    ncand = (SUBLANES + 2) * SUBLANES
    shape = (n_heads, t // LANES, nk, LANES)
    ospec = pl.BlockSpec((1, nchunk, nk, LANES), lambda h, i: (h, i, 0, 0))
    return pl.pallas_call(
        _topk_kernel,
        grid=(n_heads, t // tl),
        in_specs=[pl.BlockSpec((2, nk, tl), lambda h, i: (h, 0, i))],
        out_specs=[ospec] * 4,
        out_shape=[jax.ShapeDtypeStruct(shape, jnp.float32)] * 4,
        scratch_shapes=[pltpu.VMEM((nk, 2 * tl), jnp.float32), pltpu.VMEM((ncand, tl), jnp.float32)],
        compiler_params=_cparams("parallel", "parallel"),
        name="peer_topk",
    )(scores)


def _peer_hidden_kernel(u_ref, xn_ref, o_ref, ub_ref):
    @pl.when(pl.program_id(1) == 0)
    def _():
        ub_ref[...] = u_ref[...].astype(jnp.bfloat16)

    hid = lax.dot_general(ub_ref[...], xn_ref[...], NT_DIMS, preferred_element_type=jnp.float32)
    o_ref[...] = hid * (1.0 + lax.erf(hid * INV_SQRT2))


def peer_hidden(xn, u, *, tb=1024, eb=1024):
    t, d = xn.shape
    e = u.shape[0]
    tb = min(tb, t)
    assert t % tb == 0 and e % eb == 0
    return pl.pallas_call(
        _peer_hidden_kernel,
        grid=(e // eb, t // tb),
        in_specs=[
            pl.BlockSpec((eb, d), lambda j, i: (j, 0)),
            pl.BlockSpec((tb, d), lambda j, i: (i, 0)),
        ],
        out_specs=pl.BlockSpec((eb, tb), lambda j, i: (j, i)),
        out_shape=jax.ShapeDtypeStruct((e, t), jnp.float32),
        scratch_shapes=[pltpu.VMEM((eb, d), jnp.bfloat16)],
        compiler_params=_cparams("parallel", "arbitrary"),
        name="peer_hidden",
    )(u, xn)


GATE_GROUP = 8


def _peer_dense_kernel(h_ref, vt_ref, r2_ref, e2_ref, ell_ref, w_ref, x_ref, gain_ref, o_ref,
                       acc_ref, a_ref, *, final_norm):
    n_heads, nchunk, nk, _ = r2_ref.shape
    na = ell_ref.shape[2]
    j = pl.program_id(1)

    @pl.when(j == 0)
    def _():
        acc_ref[...] = jnp.zeros_like(acc_ref)

    hb = nk // 2
    bf = jnp.bfloat16
    for a0 in range(0, na, GATE_GROUP):
        group = range(a0, a0 + GATE_GROUP)
        for c in range(nchunk):
            lanes = slice(c * LANES, (c + 1) * LANES)
            for half in range(2):
                keys = slice(half * hb, (half + 1) * hb)
                g = {ai: None for ai in group}
                for h in range(n_heads):
                    r2 = r2_ref[h, c, keys, :].astype(bf)
                    e2 = e2_ref[h, c, keys, :].astype(bf)
                    for ai in group:
                        lim = jnp.broadcast_to(ell_ref[h, c, ai:ai + 1, :], (hb, LANES)).astype(bf)
                        wa = jnp.broadcast_to(w_ref[h, c, ai:ai + 1, :], (hb, LANES)).astype(bf)
                        term = jnp.where(r2 < lim, e2, jnp.zeros_like(e2)) * wa
                        g[ai] = term if g[ai] is None else g[ai] + term
                for ai in group:
                    rows = slice(ai * nk + half * hb, ai * nk + (half + 1) * hb)
                    act = h_ref[rows, lanes]
                    a_ref[rows, lanes] = act.astype(bf) * g[ai]
    acc_ref[...] += jnp.dot(vt_ref[...], a_ref[...], preferred_element_type=jnp.float32)

    @pl.when(j == pl.num_programs(1) - 1)
    def _():
        x = x_ref[...] + acc_ref[...].T
        if final_norm:
            inv = lax.rsqrt(jnp.mean(x * x, axis=-1, keepdims=True) + RMS_EPS)
            x = x * inv * gain_ref[...]
        o_ref[...] = x


def peer_dense(act_t, vt, r2, e2, ell, w, x, gain, *, final_norm, tb=512, na=8):
    e, t = act_t.shape
    d = vt.shape[0]
    n_heads, _, nk, _ = r2.shape
    eb = na * nk
    tb = min(tb, t)
    assert t % tb == 0 and tb % LANES == 0 and e % eb == 0 and e == nk * nk and na % SUBLANES == 0
    nchunk = tb // LANES
    key_spec = pl.BlockSpec((n_heads, nchunk, nk, LANES), lambda i, j: (0, i, 0, 0))
    row_spec = pl.BlockSpec((n_heads, nchunk, na, LANES), lambda i, j: (0, i, j, 0))
    return pl.pallas_call(
        functools.partial(_peer_dense_kernel, final_norm=final_norm),
        grid=(t // tb, e // eb),
        in_specs=[
            pl.BlockSpec((eb, tb), lambda i, j: (j, i)),
            pl.BlockSpec((d, eb), lambda i, j: (0, j)),
            key_spec, key_spec, row_spec, row_spec,
            pl.BlockSpec((tb, d), lambda i, j: (i, 0)),
            pl.BlockSpec((1, d), lambda i, j: (0, 0)),
        ],
        out_specs=pl.BlockSpec((tb, d), lambda i, j: (i, 0)),
        out_shape=jax.ShapeDtypeStruct((t, d), jnp.float32),
        scratch_shapes=[pltpu.VMEM((d, tb), jnp.float32), pltpu.VMEM((eb, tb), jnp.bfloat16)],
        compiler_params=_cparams("parallel", "arbitrary"),
        name="peer_dense",
    )(act_t, vt, r2, e2, ell, w, x, gain.reshape(1, d))


ATT_BLK = 512


def _layer(x, norm_mix, w_in, conv_w, w_conv_out, w_attn_out, w_out, norm_ffn,
           peer_wq, peer_subkeys, peer_u, peer_v, norm_final, is_last):
    t, d = x.shape
    c = conv_w.shape[1]
    a = w_attn_out.shape[0]
    n_keys, half = peer_subkeys.shape[1:]
    peer_heads = peer_wq.shape[1] // (2 * half)
    dh = LANES // 2
    attn_heads = a // dh
    bf = jnp.bfloat16

    xn = rmsnorm(x, norm_mix, bf)
    pc = matmul(xn, w_in, 0, 3 * c, jnp.float32)
    q = matmul(xn, w_in, 3 * c, a, bf, scale=dh ** -0.5)
    k = matmul(xn, w_in, 3 * c + a, a, bf)
    vt = matmul_nt(w_in[:, 3 * c + 2 * a:3 * c + 3 * a].T.astype(bf), xn, bf, tm=min(ATT_BLK, t))
    gates = matmul(xn, w_in, 3 * c + 3 * a, 2 * d, bf, sigmoid=True)

    yc = conv_mixer(pc, conv_w)
    ya = attention(q, k, vt, attn_heads, dh, blk=min(ATT_BLK, t))
    merged = merge(yc, ya, w_conv_out.astype(bf), w_attn_out.astype(bf), gates)
    x1, xn2 = out_proj(merged, w_out.astype(bf), x, norm_ffn)
    scores = peer_scores(xn2, peer_wq, peer_subkeys.astype(bf), peer_heads)
    r2, e2, ell, w = peer_topk(scores, peer_heads)
    act_t = peer_hidden(xn2, peer_u)
    return peer_dense(act_t, peer_v.T.astype(bf), r2, e2, ell, w, x1, norm_final, final_norm=is_last)


def kernel(x, norm_mix, w_in, conv_w, w_conv_out, w_attn_out, w_out, norm_ffn, peer_wq,
           peer_subkeys, peer_u, peer_v, norm_final):
    b, t, d = x.shape
    depth = norm_mix.shape[0]
    outs = []
    for bi in range(b):
        xb = x[bi]
        for layer in range(depth):
            xb = _layer(xb, norm_mix[layer], w_in[layer], conv_w[layer], w_conv_out[layer],
                        w_attn_out[layer], w_out[layer], norm_ffn[layer], peer_wq[layer],
                        peer_subkeys[layer], peer_u[layer], peer_v[layer], norm_final,
                        is_last=layer == depth - 1)
        outs.append(xb)
    return jnp.stack(outs, axis=0)
```
